```python
import jax, jax.numpy as jnp
from jax import lax
import numpy as np

D_MODEL = 1024
BATCH = 4
SEQ = 4096
DEPTH = 4

CONV_CH = 512
CONV_W = 3
MLA_HEADS = 8
QK_NOPE = 64
QK_ROPE = 32
V_DIM = 64
Q_RANK = 384
KV_RANK = 256
ROPE_THETA = 10000.0
ATTN_BLOCK = 128
EVEN_SPLITS = (CONV_CH, 2 * CONV_CH, 3 * CONV_CH, 3 * CONV_CH + Q_RANK, 3 * CONV_CH + Q_RANK + KV_RANK)
EVEN_IN = 3 * CONV_CH + Q_RANK + KV_RANK + QK_ROPE
EVEN_MIX = CONV_CH + MLA_HEADS * V_DIM
SG_WIDTH = 1024
SG_GROUPS = 8
SG_CHUNK = 128
N_GROUPS = 4
EXP_PER_GROUP = 8
N_EXPERTS = N_GROUPS * EXP_PER_GROUP
TOP_K = 2
D_EXPERT = 512
MOE_BLOCK = 128
PLE_DIM = 256
N_EVEN = (DEPTH + 1) // 2
N_ODD = DEPTH // 2
NORM_EPS = 1e-6
NEG_INF = -1e30

kernel_name = "hybrid_conv_mla_sgmlp_hmoe"


def rms_norm(x, g):
    x32 = x.astype(jnp.float32)
    y = x32 * lax.rsqrt(jnp.mean(x32 * x32, axis=-1, keepdims=True) + NORM_EPS)
    return y.astype(x.dtype) * g


def rope_angles(positions):
    inv_freq = 1.0 / (ROPE_THETA ** (jnp.arange(0, QK_ROPE, 2, dtype=jnp.float32) / QK_ROPE))
    ang = positions.astype(jnp.float32)[..., None] * inv_freq
    return jnp.cos(ang)[:, :, None, :], jnp.sin(ang)[:, :, None, :]


def apply_rope(x, cos, sin):
    half = x.shape[-1] // 2
    x1, x2 = x[..., :half], x[..., half:]
    c, s = cos.astype(x.dtype), sin.astype(x.dtype)
    return jnp.concatenate([x1 * c - x2 * s, x2 * c + x1 * s], axis=-1)


def causal_block_attention(q, k, v):
    B_, S_, H, Dqk = q.shape
    nb = S_ // ATTN_BLOCK
    scale = 1.0 / float(np.sqrt(Dqk))
    qb = q.reshape(B_, nb, ATTN_BLOCK, H, Dqk).transpose(1, 0, 2, 3, 4)
    key_idx = jnp.arange(S_)

    def attend(args):
        q_blk, blk = args
        s = jnp.einsum('bqhd,bkhd->bhqk', q_blk, k).astype(jnp.float32) * scale
        q_idx = blk * ATTN_BLOCK + jnp.arange(ATTN_BLOCK)
        s = jnp.where(key_idx[None, :] <= q_idx[:, None], s, NEG_INF)
        pr = jax.nn.softmax(s, axis=-1).astype(v.dtype)
        return jnp.einsum('bhqk,bkhd->bqhd', pr, v)

    o = lax.map(attend, (qb, jnp.arange(nb)))
    return o.transpose(1, 0, 2, 3, 4).reshape(B_, S_, H * v.shape[-1])


def conv_mla_mixer(hn, positions, w_in, conv_w, q_norm, w_q_up, kv_norm, w_kv_up, w_out):
    B_, S_, _ = hn.shape
    proj = hn @ w_in
    xc, gb, gc, c_q, c_kv, k_pe = jnp.split(proj, EVEN_SPLITS, axis=-1)
    z = gc * xc
    z = lax.conv_general_dilated(z, conv_w[:, None, :], window_strides=(1,),
                                 padding=[(CONV_W - 1, 0)],
                                 dimension_numbers=('NWC', 'WIO', 'NWC'),
                                 feature_group_count=CONV_CH)
    y_a = gb * z
    q = (rms_norm(c_q, q_norm) @ w_q_up).reshape(B_, S_, MLA_HEADS, QK_NOPE + QK_ROPE)
    kv = (rms_norm(c_kv, kv_norm) @ w_kv_up).reshape(B_, S_, MLA_HEADS, QK_NOPE + V_DIM)
    k_nope, v = kv[..., :QK_NOPE], kv[..., QK_NOPE:]
    cos, sin = rope_angles(positions)
    q_pe = apply_rope(q[..., QK_NOPE:], cos, sin)
    k_pe = apply_rope(k_pe[:, :, None, :], cos, sin)
    q = jnp.concatenate([q[..., :QK_NOPE], q_pe], axis=-1)
    k = jnp.concatenate([k_nope, jnp.broadcast_to(k_pe, (B_, S_, MLA_HEADS, QK_ROPE))], axis=-1)
    y_b = causal_block_attention(q, k, v)
    return jnp.concatenate([y_a, y_b], axis=-1) @ w_out


def spatial_gating_mixer(hn, w_in, v_norm, w_s, b_s, w_out):
    B_, S_, _ = hn.shape
    z = jax.nn.gelu(hn @ w_in)
    u, v = z[..., :SG_WIDTH], z[..., SG_WIDTH:]
    v = rms_norm(v, v_norm).reshape(B_, S_ // SG_CHUNK, SG_CHUNK, SG_GROUPS, SG_WIDTH // SG_GROUPS)
    w = jnp.tril(w_s)
    s = jnp.einsum('gts,bcsgd->bctgd', w, v) + b_s.T[:, :, None]
    return (u * s.reshape(B_, S_, SG_WIDTH)) @ w_out


def hierarchical_moe(h, w_rg, w_re, w_gate, w_up, w_down):
    B_, S_, D = h.shape
    T = B_ * S_
    xt = h.reshape(T, D)
    g_logits = (xt @ w_rg).astype(jnp.float32)
    g_prob = jax.nn.softmax(g_logits, axis=-1)
    g_idx = jnp.argmax(g_logits, axis=-1)
    g_w = jnp.take_along_axis(g_prob, g_idx[:, None], axis=-1)
    e_logits = (xt @ w_re).astype(jnp.float32).reshape(T, N_GROUPS, EXP_PER_GROUP)
    e_logits = jnp.take_along_axis(e_logits, g_idx[:, None, None], axis=1)[:, 0]
    top_p, top_i = lax.top_k(jax.nn.softmax(e_logits, axis=-1), TOP_K)
    top_p = top_p / jnp.sum(top_p, axis=-1, keepdims=True)
    weights = (g_w * top_p).reshape(-1)
    expert = (g_idx[:, None] * EXP_PER_GROUP + top_i).reshape(-1)
    tok = jnp.repeat(jnp.arange(T, dtype=jnp.int32), TOP_K)
    A = T * TOP_K
    order = jnp.argsort(expert)
    se = expert[order]
    counts = jnp.bincount(expert, length=N_EXPERTS)
    starts = jnp.cumsum(counts) - counts
    padded = ((counts + MOE_BLOCK - 1) // MOE_BLOCK) * MOE_BLOCK
    pend = jnp.cumsum(padded)
    pstarts = pend - padded
    dest = pstarts[se] + (jnp.arange(A) - starts[se])
    P = A + N_EXPERTS * MOE_BLOCK
    nblk = P // MOE_BLOCK
    buf_tok = jnp.full((P,), T, dtype=jnp.int32).at[dest].set(tok[order])
    buf_w = jnp.zeros((P,), dtype=jnp.float32).at[dest].set(weights[order])
    blk_e = jnp.minimum(jnp.searchsorted(pend, jnp.arange(nblk) * MOE_BLOCK, side='right'),
                        N_EXPERTS - 1)
    xpad = jnp.concatenate([xt, jnp.zeros((1, D), xt.dtype)], axis=0)
    xb = xpad[buf_tok].reshape(nblk, MOE_BLOCK, D)

    def expert_block(args):
        x_blk, e = args
        return (jax.nn.silu(x_blk @ w_gate[e]) * (x_blk @ w_up[e])) @ w_down[e]

    yb = lax.map(expert_block, (xb, blk_e)).reshape(P, D)
    yb = yb * buf_w[:, None].astype(yb.dtype)
    out = jax.ops.segment_sum(yb, buf_tok, num_segments=T + 1)[:T]
    return out.reshape(B_, S_, D)


def per_layer_embedding(h, p_i, g_norm, w_gate, w_proj):
    gate = jax.nn.sigmoid(rms_norm(h, g_norm) @ w_gate)
    return gate * (p_i @ w_proj)


def setup_inputs(seed: int = 0) -> dict:
    key = jax.random.key(seed)
    ks = jax.random.split(key, 32)
    f32 = jnp.float32

    def nrm(k, shape, fan_in):
        return jax.random.normal(k, shape, f32) * (fan_in ** -0.5)

    def gain(k, shape):
        return 1.0 + 0.02 * jax.random.normal(k, shape, f32)

    x = jax.random.normal(ks[0], (BATCH, SEQ, D_MODEL), f32)
    p = jax.random.normal(ks[1], (DEPTH, BATCH, SEQ, PLE_DIM), f32)
    positions = (jnp.arange(SEQ, dtype=jnp.int32)[None, :]
                 + jax.random.randint(ks[2], (BATCH, 1), 0, 1024, dtype=jnp.int32))
    return {
        'x': x,
        'p': p,
        'positions': positions,
        'norm_mix': gain(ks[3], (DEPTH, D_MODEL)),
        'norm_ffn': gain(ks[4], (DEPTH, D_MODEL)),
        'w_in_e': nrm(ks[5], (N_EVEN, D_MODEL, EVEN_IN), D_MODEL),
        'conv_w': nrm(ks[6], (N_EVEN, CONV_W, CONV_CH), CONV_W),
        'q_norm': gain(ks[7], (N_EVEN, Q_RANK)),
        'w_q_up': nrm(ks[8], (N_EVEN, Q_RANK, MLA_HEADS * (QK_NOPE + QK_ROPE)), Q_RANK),
        'kv_norm': gain(ks[9], (N_EVEN, KV_RANK)),
        'w_kv_up': nrm(ks[10], (N_EVEN, KV_RANK, MLA_HEADS * (QK_NOPE + V_DIM)), KV_RANK),
        'w_out_e': nrm(ks[11], (N_EVEN, EVEN_MIX, D_MODEL), EVEN_MIX),
        'w_in_o': nrm(ks[12], (N_ODD, D_MODEL, 2 * SG_WIDTH), D_MODEL),
        'v_norm': gain(ks[13], (N_ODD, SG_WIDTH)),
        'w_s': nrm(ks[14], (N_ODD, SG_GROUPS, SG_CHUNK, SG_CHUNK), SG_CHUNK),
        'b_s': 1.0 + 0.02 * jax.random.normal(ks[15], (N_ODD, SG_GROUPS, SG_CHUNK), f32),
        'w_out_o': nrm(ks[16], (N_ODD, SG_WIDTH, D_MODEL), SG_WIDTH),
        'w_router_group': nrm(ks[17], (DEPTH, D_MODEL, N_GROUPS), D_MODEL),
        'w_router_expert': nrm(ks[18], (DEPTH, D_MODEL, N_EXPERTS), D_MODEL),
        'w_gate': nrm(ks[19], (DEPTH, N_EXPERTS, D_MODEL, D_EXPERT), D_MODEL),
        'w_up': nrm(ks[20], (DEPTH, N_EXPERTS, D_MODEL, D_EXPERT), D_MODEL),
        'w_down': nrm(ks[21], (DEPTH, N_EXPERTS, D_EXPERT, D_MODEL), D_EXPERT),
        'norm_ple': gain(ks[22], (DEPTH, D_MODEL)),
        'w_ple_gate': nrm(ks[23], (DEPTH, D_MODEL, D_MODEL), D_MODEL),
        'w_ple_proj': nrm(ks[24], (DEPTH, PLE_DIM, D_MODEL), PLE_DIM),
        'final_norm': gain(ks[25], (D_MODEL,)),
    }


def reference(x, p, positions, norm_mix, norm_ffn, w_in_e, conv_w, q_norm, w_q_up, kv_norm,
              w_kv_up, w_out_e, w_in_o, v_norm, w_s, b_s, w_out_o, w_router_group,
              w_router_expert, w_gate, w_up, w_down, norm_ple, w_ple_gate, w_ple_proj,
              final_norm):
    h = x
    for i in range(DEPTH):
        hn = rms_norm(h, norm_mix[i])
        j = i // 2
        if i % 2 == 0:
            mix = conv_mla_mixer(hn, positions, w_in_e[j], conv_w[j], q_norm[j], w_q_up[j],
                                 kv_norm[j], w_kv_up[j], w_out_e[j])
        else:
            mix = spatial_gating_mixer(hn, w_in_o[j], v_norm[j], w_s[j], b_s[j], w_out_o[j])
        h = h + mix
        h = h + hierarchical_moe(rms_norm(h, norm_ffn[i]), w_router_group[i], w_router_expert[i],
                                 w_gate[i], w_up[i], w_down[i])
        h = h + per_layer_embedding(h, p[i], norm_ple[i], w_ple_gate[i], w_ple_proj[i])
    return rms_norm(h, final_norm)
```

```python
import functools
import math

import jax
import jax.numpy as jnp
from jax import lax
from jax.experimental import pallas as pl
from jax.experimental.pallas import tpu as pltpu

F32 = jnp.float32
BF16 = jnp.bfloat16

D_MODEL = 1024
BATCH = 4
SEQ = 4096
DEPTH = 4
TOKENS = BATCH * SEQ

CONV_CH = 512
MLA_HEADS = 8
QK_NOPE = 64
QK_ROPE = 32
V_DIM = 64
Q_RANK = 384
KV_RANK = 256
ROPE_THETA = 10000.0
SG_WIDTH = 1024
SG_GROUPS = 8
SG_CHUNK = 128
N_GROUPS = 4
EXP_PER_GROUP = 8
N_EXPERTS = N_GROUPS * EXP_PER_GROUP
D_EXPERT = 512
PLE_DIM = 256
NORM_EPS = 1e-6
NEG_INF = -1e30

LANES = 128
HEAD_PAD = LANES
ROPE_LO = QK_NOPE
ROPE_HALF = QK_ROPE // 2
EVEN_IN_PAD = 3 * CONV_CH + Q_RANK + KV_RANK + HEAD_PAD

TS = 256
TQ = 512
BM = 256
TD = 1024
ASSIGN = 2 * TOKENS
P_ROWS = ASSIGN + N_EXPERTS * BM
N_BLK = P_ROWS // BM
ROUTE_LO = N_GROUPS

VMEM_LIMIT = 48 * 1024 * 1024

Q_SCALE = math.log2(math.e) / math.sqrt(QK_NOPE + QK_ROPE)


def _cparams(n_axes=1):
    return pltpu.CompilerParams(dimension_semantics=("arbitrary",) * n_axes,
                                vmem_limit_bytes=VMEM_LIMIT)


def _rms(x, g):
    return x * lax.rsqrt(jnp.mean(x * x, axis=-1, keepdims=True) + NORM_EPS) * g


def _full(shape):
    return pl.BlockSpec(shape, lambda *_: (0,) * len(shape))


def _rope_table_kernel(pos_ref, freq_ref, c_ref, s_ref):
    ang = pos_ref[...].astype(F32) * freq_ref[...]
    lane = lax.broadcasted_iota(jnp.int32, ang.shape, 1)
    cosv = jnp.cos(ang)
    sinv = jnp.sin(ang)
    in_rope = (lane >= ROPE_LO) & (lane < ROPE_LO + QK_ROPE)
    first_half = lane < ROPE_LO + ROPE_HALF
    c_ref[...] = jnp.where(lane < ROPE_LO, 1.0, jnp.where(in_rope, cosv, 0.0))
    s_ref[...] = jnp.where(in_rope, jnp.where(first_half, -sinv, sinv), 0.0)


def _rope_tables(positions):
    inv_freq = 1.0 / (ROPE_THETA ** (jnp.arange(0, QK_ROPE, 2, dtype=F32) / QK_ROPE))
    freq = jnp.zeros((LANES,), F32).at[ROPE_LO:ROPE_LO + QK_ROPE].set(jnp.tile(inv_freq, 2))
    pos = positions.reshape(TOKENS, 1)
    return pl.pallas_call(
        _rope_table_kernel,
        grid=(TOKENS // TS,),
        in_specs=[pl.BlockSpec((TS, 1), lambda i: (i, 0)), _full((1, LANES))],
        out_specs=[pl.BlockSpec((TS, LANES), lambda i: (i, 0))] * 2,
        out_shape=[jax.ShapeDtypeStruct((TOKENS, LANES), F32)] * 2,
        compiler_params=_cparams(),
        name="rope_tables",
    )(pos, freq.reshape(1, LANES))


def _rope(x, c, s, first_half):
    partner = jnp.where(first_half, pltpu.roll(x, LANES - ROPE_HALF, 1), pltpu.roll(x, ROPE_HALF, 1))
    return x * c + partner * s


def _even_front_kernel(h_ref, g_ref, win_ref, cw_ref, qn_ref, wq_ref, kvn_ref, wkv_ref, c_ref, s_ref,
                       ya_ref, q_ref, k_ref, v_ref, zbuf):
    i = pl.program_id(0)
    hn = _rms(h_ref[...], g_ref[...]).astype(BF16)
    proj = jnp.dot(hn, win_ref[...], preferred_element_type=F32)
    xc = proj[:, 0:CONV_CH]
    gb = proj[:, CONV_CH:2 * CONV_CH]
    gc = proj[:, 2 * CONV_CH:3 * CONV_CH]
    o = 3 * CONV_CH
    cq = proj[:, o:o + Q_RANK]
    ckv = proj[:, o + Q_RANK:o + Q_RANK + KV_RANK]
    kpe = proj[:, o + Q_RANK + KV_RANK:]

    @pl.when(i % (SEQ // TS) == 0)
    def _():
        zbuf[0:8, :] = jnp.zeros((8, CONV_CH), F32)

    z = gc * xc
    zbuf[8:8 + TS, :] = z
    z1 = zbuf[7:7 + TS, :]
    z2 = zbuf[6:6 + TS, :]
    cw = cw_ref[...]
    conv = cw[0:1, :] * z2 + cw[1:2, :] * z1 + cw[2:3, :] * z
    ya_ref[...] = (gb * conv).astype(BF16)
    zbuf[0:8, :] = zbuf[TS:TS + 8, :]

    c = c_ref[...]
    s = s_ref[...]
    lane = lax.broadcasted_iota(jnp.int32, (TS, LANES), 1)
    first_half = lane < ROPE_LO + ROPE_HALF

    cqn = _rms(cq, qn_ref[...]).astype(BF16)
    q = jnp.dot(cqn, wq_ref[...], preferred_element_type=F32)
    for hd in range(MLA_HEADS):
        sl = slice(hd * HEAD_PAD, (hd + 1) * HEAD_PAD)
        q_ref[:, sl] = (_rope(q[:, sl], c, s, first_half) * Q_SCALE).astype(BF16)

    ckvn = _rms(ckv, kvn_ref[...]).astype(BF16)
    kv = jnp.dot(ckvn, wkv_ref[...], preferred_element_type=F32)
    kper = _rope(kpe, c, s, first_half)
    for hd in range(MLA_HEADS):
        sl = slice(hd * HEAD_PAD, (hd + 1) * HEAD_PAD)
        k_ref[:, sl] = (kv[:, sl] + kper).astype(BF16)
    v_ref[...] = kv[:, MLA_HEADS * HEAD_PAD:].astype(BF16)


def _even_front(h, g, win, cw, qn, wq, kvn, wkv, ctab, stab):
    tok = lambda w: pl.BlockSpec((TS, w), lambda i: (i, 0))
    return pl.pallas_call(
        _even_front_kernel,
        grid=(TOKENS // TS,),
        in_specs=[tok(D_MODEL), _full((1, D_MODEL)), _full(win.shape), _full(cw.shape),
                  _full((1, Q_RANK)), _full(wq.shape), _full((1, KV_RANK)), _full(wkv.shape),
                  tok(LANES), tok(LANES)],
        out_specs=[tok(CONV_CH), tok(MLA_HEADS * HEAD_PAD), tok(MLA_HEADS * HEAD_PAD), tok(MLA_HEADS * V_DIM)],
        out_shape=[jax.ShapeDtypeStruct((TOKENS, CONV_CH), BF16),
                   jax.ShapeDtypeStruct((TOKENS, MLA_HEADS * HEAD_PAD), BF16),
                   jax.ShapeDtypeStruct((TOKENS, MLA_HEADS * HEAD_PAD), BF16),
                   jax.ShapeDtypeStruct((TOKENS, MLA_HEADS * V_DIM), BF16)],
        scratch_shapes=[pltpu.VMEM((TS + 8, CONV_CH), F32)],
        compiler_params=_cparams(),
        name="even_front",
    )(h, g, win, cw, qn, wq, kvn, wkv, ctab, stab)


def _attn_kernel(q_ref, k_ref, v_ref, o_ref):
    i = pl.program_id(2)
    row = lax.broadcasted_iota(jnp.int32, (TQ, TQ), 0)
    col = lax.broadcasted_iota(jnp.int32, (TQ, TQ), 1)
    causal = col <= row
    nt = (((1,), (1,)), ((), ()))
    for hh in range(2):
        hs = slice(hh * HEAD_PAD, (hh + 1) * HEAD_PAD)
        q = q_ref[:, hs]

        def step(j, carry, masked):
            m, l, acc = carry
            kb = k_ref[pl.ds(pl.multiple_of(j * TQ, TQ), TQ), hs]
            vb = v_ref[pl.ds(pl.multiple_of(j * TQ, TQ), TQ), :]
            s = lax.dot_general(q, kb, nt, preferred_element_type=F32)
            if masked:
                s = jnp.where(causal, s, NEG_INF)
            m_new = jnp.maximum(m, jnp.max(s, axis=-1, keepdims=True))
            alpha = jnp.exp2(m - m_new)
            p = jnp.exp2(s - m_new)
            l = alpha * l + jnp.sum(p, axis=-1, keepdims=True)
            acc = alpha * acc + jnp.dot(p.astype(BF16), vb, preferred_element_type=F32)
            return m_new, l, acc

        init = (jnp.full((TQ, 1), NEG_INF, F32), jnp.zeros((TQ, 1), F32), jnp.zeros((TQ, 2 * V_DIM), F32))
        carry = lax.fori_loop(0, i, lambda j, cr: step(j, cr, False), init)
        m, l, acc = step(i, carry, True)
        vs = slice(hh * V_DIM, (hh + 1) * V_DIM)
        o_ref[:, vs] = (acc / l)[:, vs].astype(BF16)


def _attention(q, k, v):
    nq = SEQ // TQ
    pairs = MLA_HEADS // 2
    return pl.pallas_call(
        _attn_kernel,
        grid=(BATCH, pairs, nq),
        in_specs=[pl.BlockSpec((TQ, 2 * HEAD_PAD), lambda b, p, i: (b * nq + i, p)),
                  pl.BlockSpec((SEQ, 2 * HEAD_PAD), lambda b, p, i: (b, p)),
                  pl.BlockSpec((SEQ, 2 * V_DIM), lambda b, p, i: (b, p))],
        out_specs=pl.BlockSpec((TQ, 2 * V_DIM), lambda b, p, i: (b * nq + i, p)),
        out_shape=jax.ShapeDtypeStruct((TOKENS, MLA_HEADS * V_DIM), BF16),
        compiler_params=_cparams(3),
        name="mla_attention",
    )(q, k, v)


def _route(h1, gffn, wr_hi, wr_lo, carry, xn_ref, route_ref, cnt_ref):
    xn = _rms(h1, gffn)
    xn_ref[...] = xn
    x_hi = xn.astype(BF16)
    x_lo = (xn - x_hi.astype(F32)).astype(BF16)
    logits = (jnp.dot(x_hi, wr_hi, preferred_element_type=F32)
              + jnp.dot(x_lo, wr_hi, preferred_element_type=F32)
              + jnp.dot(x_hi, wr_lo, preferred_element_type=F32))
    lane = lax.broadcasted_iota(jnp.int32, logits.shape, 1).astype(F32)
    big = float(LANES)

    is_g = lane < N_GROUPS
    gl = jnp.where(is_g, logits, NEG_INF)
    gmax = jnp.max(gl, axis=-1, keepdims=True)
    gidx = jnp.min(jnp.where(gl == gmax, lane, big), axis=-1, keepdims=True)
    gsum = jnp.sum(jnp.where(is_g, jnp.exp(gl - gmax), 0.0), axis=-1, keepdims=True)
    g_w = 1.0 / gsum

    lo = ROUTE_LO + gidx * EXP_PER_GROUP
    el = jnp.where((lane >= lo) & (lane < lo + EXP_PER_GROUP), logits, NEG_INF)
    m1 = jnp.max(el, axis=-1, keepdims=True)
    i1 = jnp.min(jnp.where(el == m1, lane, big), axis=-1, keepdims=True)
    el2 = jnp.where(lane == i1, NEG_INF, el)
    m2 = jnp.max(el2, axis=-1, keepdims=True)
    i2 = jnp.min(jnp.where(el2 == m2, lane, big), axis=-1, keepdims=True)
    t = jnp.exp(m2 - m1)
    w1 = g_w / (1.0 + t)
    w2 = g_w * t / (1.0 + t)

    oh1 = lane == i1
    oh2 = lane == i2
    oh = jnp.where(oh1 | oh2, 1.0, 0.0)
    r = lax.broadcasted_iota(jnp.int32, (TS, TS), 0)
    cc = lax.broadcasted_iota(jnp.int32, (TS, TS), 1)
    tri = jnp.where(cc < r, 1.0, 0.0).astype(BF16)
    before = jnp.dot(tri, oh.astype(BF16), preferred_element_type=F32) + carry[...]
    rank1 = jnp.sum(jnp.where(oh1, before, 0.0), axis=-1, keepdims=True)
    rank2 = jnp.sum(jnp.where(oh2, before, 0.0), axis=-1, keepdims=True)
    carry[...] = carry[...] + jnp.sum(oh, axis=0, keepdims=True)
    cnt_ref[...] = carry[...]

    vals = (i1 - ROUTE_LO, i2 - ROUTE_LO, w1, w2, rank1, rank2)
    out = jnp.zeros_like(logits)
    for n, val in enumerate(vals):
        out = jnp.where(lane == float(n), val, out)
    route_ref[...] = out


_ROUTE_OUT_SHAPES = [jax.ShapeDtypeStruct((TOKENS, D_MODEL), F32),
                     jax.ShapeDtypeStruct((TOKENS, D_MODEL), F32),
                     jax.ShapeDtypeStruct((TOKENS, LANES), F32),
                     jax.ShapeDtypeStruct((1, LANES), F32)]


def _route_out_specs():
    tok = lambda w: pl.BlockSpec((TS, w), lambda i: (i, 0))
    return [tok(D_MODEL), tok(D_MODEL), tok(LANES), _full((1, LANES))]


def _even_out_kernel(ya_ref, yb_ref, wo_ref, h_ref, gffn_ref, wrh_ref, wrl_ref,
                     h1_ref, xn_ref, route_ref, cnt_ref, carry):
    @pl.when(pl.program_id(0) == 0)
    def _():
        carry[...] = jnp.zeros_like(carry)

    mix = (jnp.dot(ya_ref[...], wo_ref[0:CONV_CH, :], preferred_element_type=F32)
           + jnp.dot(yb_ref[...], wo_ref[CONV_CH:, :], preferred_element_type=F32))
    h1 = h_ref[...] + mix
    h1_ref[...] = h1
    _route(h1, gffn_ref[...], wrh_ref[...], wrl_ref[...], carry, xn_ref, route_ref, cnt_ref)


def _even_out(ya, yb, wo, h, gffn, wrh, wrl):
    tok = lambda w: pl.BlockSpec((TS, w), lambda i: (i, 0))
    return pl.pallas_call(
        _even_out_kernel,
        grid=(TOKENS // TS,),
        in_specs=[tok(CONV_CH), tok(MLA_HEADS * V_DIM), _full(wo.shape), tok(D_MODEL),
                  _full((1, D_MODEL)), _full(wrh.shape), _full(wrl.shape)],
        out_specs=_route_out_specs(),
        out_shape=_ROUTE_OUT_SHAPES,
        scratch_shapes=[pltpu.VMEM((1, LANES), F32)],
        compiler_params=_cparams(),
        name="even_out_route",
    )(ya, yb, wo, h, gffn, wrh, wrl)


def _gelu_tanh(x):
    return 0.5 * x * (1.0 + jnp.tanh(math.sqrt(2.0 / math.pi) * (x + 0.044715 * (x * x * x))))


def _odd_kernel(h_ref, g_ref, win_ref, vn_ref, ws_ref, bs_ref, wo_ref, gffn_ref, wrh_ref, wrl_ref,
                h1_ref, xn_ref, route_ref, cnt_ref, carry, mixbuf):
    @pl.when(pl.program_id(0) == 0)
    def _():
        carry[...] = jnp.zeros_like(carry)

    h = h_ref[...]
    hn = _rms(h, g_ref[...]).astype(BF16)
    z = _gelu_tanh(jnp.dot(hn, win_ref[...], preferred_element_type=F32))
    u = z[:, :SG_WIDTH]
    v = _rms(z[:, SG_WIDTH:], vn_ref[...]).astype(BF16)
    r = lax.broadcasted_iota(jnp.int32, (SG_CHUNK, SG_CHUNK), 0)
    c = lax.broadcasted_iota(jnp.int32, (SG_CHUNK, SG_CHUNK), 1)
    n_chunks = TS // SG_CHUNK
    dg = SG_WIDTH // SG_GROUPS
    for gi in range(SG_GROUPS):
        gs = slice(gi * dg, (gi + 1) * dg)
        wtril = jnp.where(c <= r, ws_ref[gi], 0.0).astype(BF16)
        rhs = jnp.concatenate([v[ci * SG_CHUNK:(ci + 1) * SG_CHUNK, gs] for ci in range(n_chunks)], axis=1)
        sg = jnp.dot(wtril, rhs, preferred_element_type=F32)
        for ci in range(n_chunks):
            rs = slice(ci * SG_CHUNK, (ci + 1) * SG_CHUNK)
            gate = sg[:, ci * dg:(ci + 1) * dg] + bs_ref[:, gs]
            mixbuf[rs, gs] = (u[rs, gs] * gate).astype(BF16)
    h1 = h + jnp.dot(mixbuf[...], wo_ref[...], preferred_element_type=F32)
    h1_ref[...] = h1
    _route(h1, gffn_ref[...], wrh_ref[...], wrl_ref[...], carry, xn_ref, route_ref, cnt_ref)


def _odd_layer(h, g, win, vn, ws, bs_full, wo, gffn, wrh, wrl):
    tok = lambda w: pl.BlockSpec((TS, w), lambda i: (i, 0))
    return pl.pallas_call(
        _odd_kernel,
        grid=(TOKENS // TS,),
        in_specs=[tok(D_MODEL), _full((1, D_MODEL)), _full(win.shape), _full((1, SG_WIDTH)),
                  _full(ws.shape), _full(bs_full.shape), _full(wo.shape),
                  _full((1, D_MODEL)), _full(wrh.shape), _full(wrl.shape)],
        out_specs=_route_out_specs(),
        out_shape=_ROUTE_OUT_SHAPES,
        scratch_shapes=[pltpu.VMEM((1, LANES), F32), pltpu.VMEM((TS, SG_WIDTH), BF16)],
        compiler_params=_cparams(),
        name="odd_mixer_route",
    )(h, g, win, vn, ws, bs_full, wo, gffn, wrh, wrl)


def _row_copy(src, si, dst, di, sem):
    return pltpu.make_async_copy(src.at[pl.ds(si, 1)], dst.at[pl.ds(di, 1)], sem)


def _dispatch_kernel(dest_ref, cnt_ref, pad_ref, pst_ref, nblk_ref, xn_ref, xs_hbm, zblk, sem, zsem):
    i = pl.program_id(0)

    def issue(r, _):
        t = i * TD + r
        _row_copy(xn_ref, r, xs_hbm, dest_ref[t], sem).start()
        _row_copy(xn_ref, r, xs_hbm, dest_ref[TOKENS + t], sem).start()
        return 0

    lax.fori_loop(0, TD, issue, 0, unroll=8)

    @pl.when(i == pl.num_programs(0) - 1)
    def _():
        zblk[...] = jnp.zeros_like(zblk)

        def per_expert(e, _):
            base = pst_ref[e]
            lax.fori_loop(cnt_ref[e], pad_ref[e],
                          lambda r, c: (_row_copy(zblk, 0, xs_hbm, base + r, zsem).start(), c)[1], 0)
            lax.fori_loop(cnt_ref[e], pad_ref[e],
                          lambda r, c: (_row_copy(zblk, 0, xs_hbm, base + r, zsem).wait(), c)[1], 0)
            return 0

        lax.fori_loop(0, N_EXPERTS, per_expert, 0)

        def tail_copy(b):
            return pltpu.make_async_copy(zblk, xs_hbm.at[pl.ds(pl.multiple_of(b * BM, BM), BM)], zsem)

        lax.fori_loop(nblk_ref[0], N_BLK, lambda b, c: (tail_copy(b).start(), c)[1], 0)
        lax.fori_loop(nblk_ref[0], N_BLK, lambda b, c: (tail_copy(b).wait(), c)[1], 0)

    def drain(r, _):
        t = i * TD + r
        _row_copy(xn_ref, r, xs_hbm, dest_ref[t], sem).wait()
        _row_copy(xn_ref, r, xs_hbm, dest_ref[TOKENS + t], sem).wait()
        return 0

    lax.fori_loop(0, TD, drain, 0, unroll=8)


def _dispatch(dest, cnt, padded, pstart, nblk, xn):
    return pl.pallas_call(
        _dispatch_kernel,
        grid_spec=pltpu.PrefetchScalarGridSpec(
            num_scalar_prefetch=5,
            grid=(TOKENS // TD,),
            in_specs=[pl.BlockSpec((TD, D_MODEL), lambda i, *_: (i, 0))],
            out_specs=pl.BlockSpec(memory_space=pl.ANY),
            scratch_shapes=[pltpu.VMEM((BM, D_MODEL), F32), pltpu.SemaphoreType.DMA, pltpu.SemaphoreType.DMA],
        ),
        out_shape=jax.ShapeDtypeStruct((P_ROWS, D_MODEL), F32),
        compiler_params=_cparams(),
        name="moe_dispatch",
    )(dest, cnt, padded, pstart, nblk, xn)


def _gmm_kernel(blk_e_ref, nblk_ref, xs_ref, wg_ref, wu_ref, wd_ref, o_ref, wgb, wub, wdb):
    i = pl.program_id(0)

    @pl.when(i < nblk_ref[0])
    def _():
        e = blk_e_ref[i]
        prev = blk_e_ref[jnp.maximum(i - 1, 0)]

        @pl.when((i == 0) | (e != prev))
        def _():
            wgb[...] = wg_ref[0].astype(BF16)
            wub[...] = wu_ref[0].astype(BF16)
            wdb[...] = wd_ref[0].astype(BF16)

        x = xs_ref[...].astype(BF16)
        gate = jnp.dot(x, wgb[...], preferred_element_type=F32)
        up = jnp.dot(x, wub[...], preferred_element_type=F32)
        mid = (gate * jax.nn.sigmoid(gate) * up).astype(BF16)
        o_ref[...] = jnp.dot(mid, wdb[...], preferred_element_type=F32)

    @pl.when(i >= nblk_ref[0])
    def _():
        o_ref[...] = jnp.zeros_like(o_ref)


def _gmm(blk_e, nblk, xs, w_gate, w_up, w_down, layer):
    last = lambda i, nb: jnp.minimum(i, nb[0] - 1)
    wsel = lambda i, be, nb: (layer * N_EXPERTS + be[last(i, nb)], 0, 0)
    return pl.pallas_call(
        _gmm_kernel,
        grid_spec=pltpu.PrefetchScalarGridSpec(
            num_scalar_prefetch=2,
            grid=(N_BLK,),
            in_specs=[pl.BlockSpec((BM, D_MODEL), lambda i, be, nb: (last(i, nb), 0)),
                      pl.BlockSpec((1, D_MODEL, D_EXPERT), wsel),
                      pl.BlockSpec((1, D_MODEL, D_EXPERT), wsel),
                      pl.BlockSpec((1, D_EXPERT, D_MODEL), wsel)],
            out_specs=pl.BlockSpec((BM, D_MODEL), lambda i, be, nb: (i, 0)),
            scratch_shapes=[pltpu.VMEM((D_MODEL, D_EXPERT), BF16), pltpu.VMEM((D_MODEL, D_EXPERT), BF16),
                            pltpu.VMEM((D_EXPERT, D_MODEL), BF16)],
        ),
        out_shape=jax.ShapeDtypeStruct((P_ROWS, D_MODEL), F32),
        compiler_params=_cparams(),
        name="moe_grouped_matmul",
    )(blk_e, nblk, xs, w_gate, w_up, w_down)


def _combine_ple_kernel(dest_ref, h1_ref, route_ref, p_ref, gple_ref, wgate_ref, wproj_ref, fin_ref, yb_hbm,
                        out_ref, ybuf, sem, *, final):
    i = pl.program_id(0)
    n = pl.num_programs(0)

    def gather_tile(tile, slot, wait):
        def body(r, _):
            t = tile * TS + r
            c1 = _row_copy(yb_hbm, dest_ref[t], ybuf.at[slot], r, sem.at[slot])
            c2 = _row_copy(yb_hbm, dest_ref[TOKENS + t], ybuf.at[slot], TS + r, sem.at[slot])
            if wait:
                c1.wait()
                c2.wait()
            else:
                c1.start()
                c2.start()
            return 0

        lax.fori_loop(0, TS, body, 0, unroll=8)

    @pl.when(i == 0)
    def _():
        gather_tile(0, 0, False)

    @pl.when(i + 1 < n)
    def _():
        gather_tile(i + 1, (i + 1) % 2, False)

    slot = i % 2
    gather_tile(i, slot, True)

    route = route_ref[...]
    w1 = route[:, 2:3]
    w2 = route[:, 3:4]
    h2 = h1_ref[...] + (w1 * ybuf[slot, 0:TS, :] + w2 * ybuf[slot, TS:2 * TS, :])
    gate = jax.nn.sigmoid(jnp.dot(_rms(h2, gple_ref[...]).astype(BF16), wgate_ref[...],
                                  preferred_element_type=F32))
    out = h2 + gate * jnp.dot(p_ref[...].astype(BF16), wproj_ref[...], preferred_element_type=F32)
    if final:
        out = _rms(out, fin_ref[...])
    out_ref[...] = out


def _combine_ple(dest, h1, route, p_all, layer, gple, wgate, wproj, fin, yb, final):
    nt = TOKENS // TS
    tok = lambda w: pl.BlockSpec((TS, w), lambda i, *_: (i, 0))
    full = lambda shape: pl.BlockSpec(shape, lambda i, *_: (0,) * len(shape))
    return pl.pallas_call(
        functools.partial(_combine_ple_kernel, final=final),
        grid_spec=pltpu.PrefetchScalarGridSpec(
            num_scalar_prefetch=1,
            grid=(nt,),
            in_specs=[tok(D_MODEL), tok(LANES),
                      pl.BlockSpec((TS, PLE_DIM), lambda i, *_: (layer * nt + i, 0)),
                      full((1, D_MODEL)), full(wgate.shape), full(wproj.shape), full((1, D_MODEL)),
                      pl.BlockSpec(memory_space=pl.ANY)],
            out_specs=tok(D_MODEL),
            scratch_shapes=[pltpu.VMEM((2, 2 * TS, D_MODEL), F32), pltpu.SemaphoreType.DMA((2,))],
        ),
        out_shape=jax.ShapeDtypeStruct((TOKENS, D_MODEL), F32),
        compiler_params=_cparams(),
        name="moe_combine_ple",
    )(dest, h1, route, p_all, gple, wgate, wproj, fin, yb)


def _dispatch_plan(route, cnt):
    e1 = route[:, 0].astype(jnp.int32)
    e2 = route[:, 1].astype(jnp.int32)
    rank1 = route[:, 4].astype(jnp.int32)
    rank2 = route[:, 5].astype(jnp.int32)
    counts = cnt[0, ROUTE_LO:ROUTE_LO + N_EXPERTS].astype(jnp.int32)
    padded = ((counts + BM - 1) // BM) * BM
    pend = jnp.cumsum(padded)
    pstart = pend - padded
    dest = jnp.concatenate([pstart[e1] + rank1, pstart[e2] + rank2])
    nblk = (pend[-1] // BM).reshape(1)
    blk = jnp.minimum(jnp.arange(N_BLK, dtype=jnp.int32), nblk[0] - 1)
    owner = jnp.sum((pend[None, :] <= (blk * BM)[:, None]).astype(jnp.int32), axis=1)
    blk_e = jnp.minimum(owner, N_EXPERTS - 1)
    return dest, counts, padded, pstart, blk_e, nblk.astype(jnp.int32)


def _router_weights(w_rg, w_re):
    w = jnp.zeros((D_MODEL, LANES), F32).at[:, :N_GROUPS].set(w_rg).at[:, ROUTE_LO:ROUTE_LO + N_EXPERTS].set(w_re)
    hi = w.astype(BF16)
    lo = (w - hi.astype(F32)).astype(BF16)
    return hi, lo


def _even_weights(w_in, w_q_up, w_kv_up):
    base = 3 * CONV_CH + Q_RANK + KV_RANK
    kpe_cols = jnp.zeros((D_MODEL, HEAD_PAD), F32).at[:, ROPE_LO:ROPE_LO + QK_ROPE].set(w_in[:, base:])
    win = jnp.concatenate([w_in[:, :base], kpe_cols], axis=1).astype(BF16)
    dqk = QK_NOPE + QK_ROPE
    wq = jnp.pad(w_q_up.reshape(Q_RANK, MLA_HEADS, dqk), ((0, 0), (0, 0), (0, HEAD_PAD - dqk)))
    wq = wq.reshape(Q_RANK, MLA_HEADS * HEAD_PAD).astype(BF16)
    wkv = w_kv_up.reshape(KV_RANK, MLA_HEADS, QK_NOPE + V_DIM)
    wk = jnp.pad(wkv[:, :, :QK_NOPE], ((0, 0), (0, 0), (0, HEAD_PAD - QK_NOPE))).reshape(KV_RANK, MLA_HEADS * HEAD_PAD)
    wv = wkv[:, :, QK_NOPE:].reshape(KV_RANK, MLA_HEADS * V_DIM)
    return win, wq, jnp.concatenate([wk, wv], axis=1).astype(BF16)


def kernel(x, p, positions, norm_mix, norm_ffn, w_in_e, conv_w, q_norm, w_q_up, kv_norm, w_kv_up, w_out_e,
           w_in_o, v_norm, w_s, b_s, w_out_o, w_router_group, w_router_expert, w_gate, w_up, w_down,
           norm_ple, w_ple_gate, w_ple_proj, final_norm):
    h = x.reshape(TOKENS, D_MODEL)
    p_all = p.reshape(DEPTH * TOKENS, PLE_DIM)
    wg_all = w_gate.reshape(DEPTH * N_EXPERTS, D_MODEL, D_EXPERT)
    wu_all = w_up.reshape(DEPTH * N_EXPERTS, D_MODEL, D_EXPERT)
    wd_all = w_down.reshape(DEPTH * N_EXPERTS, D_EXPERT, D_MODEL)
    ctab, stab = _rope_tables(positions)
    row = lambda a: a.reshape(1, -1)

    for i in range(DEPTH):
        j = i // 2
        wrh, wrl = _router_weights(w_router_group[i], w_router_expert[i])
        if i % 2 == 0:
            win, wq, wkv = _even_weights(w_in_e[j], w_q_up[j], w_kv_up[j])
            ya, q, k, v = _even_front(h, row(norm_mix[i]), win, conv_w[j], row(q_norm[j]), wq,
                                      row(kv_norm[j]), wkv, ctab, stab)
            yb = _attention(q, k, v)
            h1, xn, route, cnt = _even_out(ya, yb, w_out_e[j].astype(BF16), h, row(norm_ffn[i]), wrh, wrl)
        else:
            bs_full = jnp.repeat(b_s[j].T, SG_WIDTH // SG_GROUPS, axis=1)
            h1, xn, route, cnt = _odd_layer(h, row(norm_mix[i]), w_in_o[j].astype(BF16), row(v_norm[j]),
                                            w_s[j], bs_full, w_out_o[j].astype(BF16), row(norm_ffn[i]), wrh, wrl)
        dest, counts, padded, pstart, blk_e, nblk = _dispatch_plan(route, cnt)
        xs = _dispatch(dest, counts, padded, pstart, nblk, xn)
        ybuf = _gmm(blk_e, nblk, xs, wg_all, wu_all, wd_all, i)
        h = _combine_ple(dest, h1, route, p_all, i, row(norm_ple[i]), w_ple_gate[i].astype(BF16),
                         w_ple_proj[i].astype(BF16), row(final_norm), ybuf, final=(i == DEPTH - 1))
    return h.reshape(BATCH, SEQ, D_MODEL)
```

```python
import functools
import math

import jax
import jax.numpy as jnp
from jax import lax
from jax.experimental import pallas as pl
from jax.experimental.pallas import tpu as pltpu

F32 = jnp.float32
BF16 = jnp.bfloat16

D_MODEL = 1024
BATCH = 4
SEQ = 4096
DEPTH = 4
TOKENS = BATCH * SEQ

CONV_CH = 512
MLA_HEADS = 8
QK_NOPE = 64
QK_ROPE = 32
V_DIM = 64
Q_RANK = 384
KV_RANK = 256
ROPE_THETA = 10000.0
SG_WIDTH = 1024
SG_GROUPS = 8
SG_CHUNK = 128
N_GROUPS = 4
EXP_PER_GROUP = 8
N_EXPERTS = N_GROUPS * EXP_PER_GROUP
D_EXPERT = 512
PLE_DIM = 256
NORM_EPS = 1e-6
NEG_INF = -1e30

LANES = 128
HEAD_PAD = LANES
ROPE_LO = QK_NOPE
ROPE_HALF = QK_ROPE // 2
EVEN_IN_PAD = 3 * CONV_CH + Q_RANK + KV_RANK + HEAD_PAD

TS = 256
TQ = 1024
BM = 256
BM_SHIFT = BM.bit_length() - 1
assert 1 << BM_SHIFT == BM
TD = 1024
ASSIGN = 2 * TOKENS
P_ROWS = ASSIGN + N_EXPERTS * BM
N_BLK = P_ROWS // BM
ROUTE_LO = N_GROUPS

VMEM_LIMIT = 48 * 1024 * 1024

Q_SCALE = math.log2(math.e) / math.sqrt(QK_NOPE + QK_ROPE)


def _cparams(n_axes=1):
    return pltpu.CompilerParams(dimension_semantics=("arbitrary",) * n_axes,
                                vmem_limit_bytes=VMEM_LIMIT)


def _rms(x, g):
    return x * lax.rsqrt(jnp.mean(x * x, axis=-1, keepdims=True) + NORM_EPS) * g


def _full(shape):
    return pl.BlockSpec(shape, lambda *_: (0,) * len(shape))


def _rope_table_kernel(pos_ref, freq_ref, c_ref, s_ref):
    ang = pos_ref[...].astype(F32) * freq_ref[...]
    lane = lax.broadcasted_iota(jnp.int32, ang.shape, 1)
    cosv = jnp.cos(ang)
    sinv = jnp.sin(ang)
    in_rope = (lane >= ROPE_LO) & (lane < ROPE_LO + QK_ROPE)
    first_half = lane < ROPE_LO + ROPE_HALF
    c_ref[...] = jnp.where(lane < ROPE_LO, 1.0, jnp.where(in_rope, cosv, 0.0))
    s_ref[...] = jnp.where(in_rope, jnp.where(first_half, -sinv, sinv), 0.0)


def _rope_tables(positions):
    inv_freq = 1.0 / (ROPE_THETA ** (jnp.arange(0, QK_ROPE, 2, dtype=F32) / QK_ROPE))
    freq = jnp.zeros((LANES,), F32).at[ROPE_LO:ROPE_LO + QK_ROPE].set(jnp.tile(inv_freq, 2))
    pos = positions.reshape(TOKENS, 1)
    return pl.pallas_call(
        _rope_table_kernel,
        grid=(TOKENS // TS,),
        in_specs=[pl.BlockSpec((TS, 1), lambda i: (i, 0)), _full((1, LANES))],
        out_specs=[pl.BlockSpec((TS, LANES), lambda i: (i, 0))] * 2,
        out_shape=[jax.ShapeDtypeStruct((TOKENS, LANES), F32)] * 2,
        compiler_params=_cparams(),
        name="rope_tables",
    )(pos, freq.reshape(1, LANES))


def _rope(x, c, s, first_half):
    partner = jnp.where(first_half, pltpu.roll(x, LANES - ROPE_HALF, 1), pltpu.roll(x, ROPE_HALF, 1))
    return x * c + partner * s


def _even_front_kernel(h_ref, g_ref, win_ref, cw_ref, qn_ref, wq_ref, kvn_ref, wkv_ref, c_ref, s_ref,
                       ya_ref, q_ref, k_ref, v_ref, zbuf):
    i = pl.program_id(0)
    hn = _rms(h_ref[...], g_ref[...]).astype(BF16)
    proj = jnp.dot(hn, win_ref[...], preferred_element_type=F32)
    xc = proj[:, 0:CONV_CH]
    gb = proj[:, CONV_CH:2 * CONV_CH]
    gc = proj[:, 2 * CONV_CH:3 * CONV_CH]
    o = 3 * CONV_CH
    cq = proj[:, o:o + Q_RANK]
    ckv = proj[:, o + Q_RANK:o + Q_RANK + KV_RANK]
    kpe = proj[:, o + Q_RANK + KV_RANK:]

    @pl.when(i % (SEQ // TS) == 0)
    def _():
        zbuf[0:8, :] = jnp.zeros((8, CONV_CH), F32)

    z = gc * xc
    zbuf[8:8 + TS, :] = z
    z1 = zbuf[7:7 + TS, :]
    z2 = zbuf[6:6 + TS, :]
    cw = cw_ref[...]
    conv = cw[0:1, :] * z2 + cw[1:2, :] * z1 + cw[2:3, :] * z
    ya_ref[...] = (gb * conv).astype(BF16)
    zbuf[0:8, :] = zbuf[TS:TS + 8, :]

    c = c_ref[...]
    s = s_ref[...]
    lane = lax.broadcasted_iota(jnp.int32, (TS, LANES), 1)
    first_half = lane < ROPE_LO + ROPE_HALF

    cqn = _rms(cq, qn_ref[...]).astype(BF16)
    q = jnp.dot(cqn, wq_ref[...], preferred_element_type=F32)
    for hd in range(MLA_HEADS):
        sl = slice(hd * HEAD_PAD, (hd + 1) * HEAD_PAD)
        q_ref[:, sl] = (_rope(q[:, sl], c, s, first_half) * Q_SCALE).astype(BF16)

    ckvn = _rms(ckv, kvn_ref[...]).astype(BF16)
    kv = jnp.dot(ckvn, wkv_ref[...], preferred_element_type=F32)
    kper = _rope(kpe, c, s, first_half)
    for hd in range(MLA_HEADS):
        sl = slice(hd * HEAD_PAD, (hd + 1) * HEAD_PAD)
        k_ref[:, sl] = (kv[:, sl] + kper).astype(BF16)
    v_ref[...] = kv[:, MLA_HEADS * HEAD_PAD:].astype(BF16)


def _even_front(h, g, win, cw, qn, wq, kvn, wkv, ctab, stab):
    tok = lambda w: pl.BlockSpec((TS, w), lambda i: (i, 0))
    return pl.pallas_call(
        _even_front_kernel,
        grid=(TOKENS // TS,),
        in_specs=[tok(D_MODEL), _full((1, D_MODEL)), _full(win.shape), _full(cw.shape),
                  _full((1, Q_RANK)), _full(wq.shape), _full((1, KV_RANK)), _full(wkv.shape),
                  tok(LANES), tok(LANES)],
        out_specs=[tok(CONV_CH), tok(MLA_HEADS * HEAD_PAD), tok(MLA_HEADS * HEAD_PAD), tok(MLA_HEADS * V_DIM)],
        out_shape=[jax.ShapeDtypeStruct((TOKENS, CONV_CH), BF16),
                   jax.ShapeDtypeStruct((TOKENS, MLA_HEADS * HEAD_PAD), BF16),
                   jax.ShapeDtypeStruct((TOKENS, MLA_HEADS * HEAD_PAD), BF16),
                   jax.ShapeDtypeStruct((TOKENS, MLA_HEADS * V_DIM), BF16)],
        scratch_shapes=[pltpu.VMEM((TS + 8, CONV_CH), F32)],
        compiler_params=_cparams(),
        name="even_front",
    )(h, g, win, cw, qn, wq, kvn, wkv, ctab, stab)


def _attn_kernel(q_ref, k_ref, v_ref, o_ref):
    i = pl.program_id(2)
    row = lax.broadcasted_iota(jnp.int32, (TQ, TQ), 0)
    col = lax.broadcasted_iota(jnp.int32, (TQ, TQ), 1)
    causal = col <= row
    nt = (((1,), (1,)), ((), ()))
    for hh in range(2):
        hs = slice(hh * HEAD_PAD, (hh + 1) * HEAD_PAD)
        q = q_ref[:, hs]

        def step(j, carry, masked):
            m, l, acc = carry
            kb = k_ref[pl.ds(pl.multiple_of(j * TQ, TQ), TQ), hs]
            vb = v_ref[pl.ds(pl.multiple_of(j * TQ, TQ), TQ), :]
            s = lax.dot_general(q, kb, nt, preferred_element_type=F32)
            if masked:
                s = jnp.where(causal, s, NEG_INF)
            m_new = jnp.maximum(m, jnp.max(s, axis=-1, keepdims=True))
            alpha = jnp.exp2(m - m_new)
            p = jnp.exp2(s - m_new)
            l = alpha * l + jnp.sum(p, axis=-1, keepdims=True)
            acc = alpha * acc + jnp.dot(p.astype(BF16), vb, preferred_element_type=F32)
            return m_new, l, acc

        init = (jnp.full((TQ, 1), NEG_INF, F32), jnp.zeros((TQ, 1), F32), jnp.zeros((TQ, 2 * V_DIM), F32))
        carry = lax.fori_loop(0, i, lambda j, cr: step(j, cr, False), init)
        m, l, acc = step(i, carry, True)
        vs = slice(hh * V_DIM, (hh + 1) * V_DIM)
        o_ref[:, vs] = (acc / l)[:, vs].astype(BF16)


def _attention(q, k, v):
    nq = SEQ // TQ
    pairs = MLA_HEADS // 2
    return pl.pallas_call(
        _attn_kernel,
        grid=(BATCH, pairs, nq),
        in_specs=[pl.BlockSpec((TQ, 2 * HEAD_PAD), lambda b, p, i: (b * nq + i, p)),
                  pl.BlockSpec((SEQ, 2 * HEAD_PAD), lambda b, p, i: (b, p)),
                  pl.BlockSpec((SEQ, 2 * V_DIM), lambda b, p, i: (b, p))],
        out_specs=pl.BlockSpec((TQ, 2 * V_DIM), lambda b, p, i: (b * nq + i, p)),
        out_shape=jax.ShapeDtypeStruct((TOKENS, MLA_HEADS * V_DIM), BF16),
        compiler_params=_cparams(3),
        name="mla_attention",
    )(q, k, v)


def _route(h1, gffn, wr_hi, wr_lo, carry, xn_ref, route_ref, lane_ref, cnt_ref):
    xn = _rms(h1, gffn)
    xn_ref[...] = xn
    x_hi = xn.astype(BF16)
    x_lo = (xn - x_hi.astype(F32)).astype(BF16)
    logits = (jnp.dot(x_hi, wr_hi, preferred_element_type=F32)
              + jnp.dot(x_lo, wr_hi, preferred_element_type=F32)
              + jnp.dot(x_hi, wr_lo, preferred_element_type=F32))
    lane = lax.broadcasted_iota(jnp.int32, logits.shape, 1).astype(F32)
    big = float(LANES)

    is_g = lane < N_GROUPS
    gl = jnp.where(is_g, logits, NEG_INF)
    gmax = jnp.max(gl, axis=-1, keepdims=True)
    gidx = jnp.min(jnp.where(gl == gmax, lane, big), axis=-1, keepdims=True)
    gsum = jnp.sum(jnp.where(is_g, jnp.exp(gl - gmax), 0.0), axis=-1, keepdims=True)
    g_w = 1.0 / gsum

    lo = ROUTE_LO + gidx * EXP_PER_GROUP
    el = jnp.where((lane >= lo) & (lane < lo + EXP_PER_GROUP), logits, NEG_INF)
    m1 = jnp.max(el, axis=-1, keepdims=True)
    i1 = jnp.min(jnp.where(el == m1, lane, big), axis=-1, keepdims=True)
    el2 = jnp.where(lane == i1, NEG_INF, el)
    m2 = jnp.max(el2, axis=-1, keepdims=True)
    i2 = jnp.min(jnp.where(el2 == m2, lane, big), axis=-1, keepdims=True)
    t = jnp.exp(m2 - m1)
    w1 = g_w / (1.0 + t)
    w2 = g_w * t / (1.0 + t)

    oh1 = lane == i1
    oh2 = lane == i2
    oh = jnp.where(oh1 | oh2, 1.0, 0.0)
    r = lax.broadcasted_iota(jnp.int32, (TS, TS), 0)
    cc = lax.broadcasted_iota(jnp.int32, (TS, TS), 1)
    tri = jnp.where(cc < r, 1.0, 0.0).astype(BF16)
    before = jnp.dot(tri, oh.astype(BF16), preferred_element_type=F32) + carry[...]
    rank1 = jnp.sum(jnp.where(oh1, before, 0.0), axis=-1, keepdims=True)
    rank2 = jnp.sum(jnp.where(oh2, before, 0.0), axis=-1, keepdims=True)
    carry[...] = carry[...] + jnp.sum(oh, axis=0, keepdims=True)
    cnt_ref[...] = carry[...].astype(jnp.int32)

    vals = (i1 - ROUTE_LO, i2 - ROUTE_LO, w1, w2, rank1, rank2)
    out = jnp.zeros_like(logits)
    for n, val in enumerate(vals):
        out = jnp.where(lane == float(n), val, out)
    route_ref[...] = out
    out_t = out.T
    rows = [out_t[n:n + 1, part * LANES:(part + 1) * LANES] for n in (0, 1, 4, 5) for part in range(TS // LANES)]
    lane_ref[0] = jnp.concatenate(rows, axis=0)


_ROUTE_OUT_SHAPES = [jax.ShapeDtypeStruct((TOKENS, D_MODEL), F32),
                     jax.ShapeDtypeStruct((TOKENS, D_MODEL), F32),
                     jax.ShapeDtypeStruct((TOKENS, LANES), F32),
                     jax.ShapeDtypeStruct((TOKENS // TS, 4 * (TS // LANES), LANES), F32),
                     jax.ShapeDtypeStruct((1, LANES), jnp.int32)]


def _route_out_specs():
    tok = lambda w: pl.BlockSpec((TS, w), lambda i: (i, 0))
    return [tok(D_MODEL), tok(D_MODEL), tok(LANES),
            pl.BlockSpec((1, 4 * (TS // LANES), LANES), lambda i: (i, 0, 0)), _full((1, LANES))]


def _even_out_kernel(ya_ref, yb_ref, wo_ref, h_ref, gffn_ref, wrh_ref, wrl_ref,
                     h1_ref, xn_ref, route_ref, lane_ref, cnt_ref, carry):
    @pl.when(pl.program_id(0) == 0)
    def _():
        carry[...] = jnp.zeros_like(carry)

    mix = (jnp.dot(ya_ref[...], wo_ref[0:CONV_CH, :], preferred_element_type=F32)
           + jnp.dot(yb_ref[...], wo_ref[CONV_CH:, :], preferred_element_type=F32))
    h1 = h_ref[...] + mix
    h1_ref[...] = h1
    _route(h1, gffn_ref[...], wrh_ref[...], wrl_ref[...], carry, xn_ref, route_ref, lane_ref, cnt_ref)


def _even_out(ya, yb, wo, h, gffn, wrh, wrl):
    tok = lambda w: pl.BlockSpec((TS, w), lambda i: (i, 0))
    return pl.pallas_call(
        _even_out_kernel,
        grid=(TOKENS // TS,),
        in_specs=[tok(CONV_CH), tok(MLA_HEADS * V_DIM), _full(wo.shape), tok(D_MODEL),
                  _full((1, D_MODEL)), _full(wrh.shape), _full(wrl.shape)],
        out_specs=_route_out_specs(),
        out_shape=_ROUTE_OUT_SHAPES,
        scratch_shapes=[pltpu.VMEM((1, LANES), F32)],
        compiler_params=_cparams(),
        name="even_out_route",
    )(ya, yb, wo, h, gffn, wrh, wrl)


def _gelu_tanh(x):
    return 0.5 * x * (1.0 + jnp.tanh(math.sqrt(2.0 / math.pi) * (x + 0.044715 * (x * x * x))))


def _odd_kernel(h_ref, g_ref, win_ref, vn_ref, ws_ref, bs_ref, wo_ref, gffn_ref, wrh_ref, wrl_ref,
                h1_ref, xn_ref, route_ref, lane_ref, cnt_ref, carry, mixbuf):
    @pl.when(pl.program_id(0) == 0)
    def _():
        carry[...] = jnp.zeros_like(carry)

    h = h_ref[...]
    hn = _rms(h, g_ref[...]).astype(BF16)
    z = _gelu_tanh(jnp.dot(hn, win_ref[...], preferred_element_type=F32))
    u = z[:, :SG_WIDTH]
    v = _rms(z[:, SG_WIDTH:], vn_ref[...]).astype(BF16)
    r = lax.broadcasted_iota(jnp.int32, (SG_CHUNK, SG_CHUNK), 0)
    c = lax.broadcasted_iota(jnp.int32, (SG_CHUNK, SG_CHUNK), 1)
    n_chunks = TS // SG_CHUNK
    dg = SG_WIDTH // SG_GROUPS
    for gi in range(SG_GROUPS):
        gs = slice(gi * dg, (gi + 1) * dg)
        wtril = jnp.where(c <= r, ws_ref[gi], 0.0).astype(BF16)
        rhs = jnp.concatenate([v[ci * SG_CHUNK:(ci + 1) * SG_CHUNK, gs] for ci in range(n_chunks)], axis=1)
        sg = jnp.dot(wtril, rhs, preferred_element_type=F32)
        for ci in range(n_chunks):
            rs = slice(ci * SG_CHUNK, (ci + 1) * SG_CHUNK)
            gate = sg[:, ci * dg:(ci + 1) * dg] + bs_ref[:, gs]
            mixbuf[rs, gs] = (u[rs, gs] * gate).astype(BF16)
    h1 = h + jnp.dot(mixbuf[...], wo_ref[...], preferred_element_type=F32)
    h1_ref[...] = h1
    _route(h1, gffn_ref[...], wrh_ref[...], wrl_ref[...], carry, xn_ref, route_ref, lane_ref, cnt_ref)


def _odd_layer(h, g, win, vn, ws, bs_full, wo, gffn, wrh, wrl):
    tok = lambda w: pl.BlockSpec((TS, w), lambda i: (i, 0))
    return pl.pallas_call(
        _odd_kernel,
        grid=(TOKENS // TS,),
        in_specs=[tok(D_MODEL), _full((1, D_MODEL)), _full(win.shape), _full((1, SG_WIDTH)),
                  _full(ws.shape), _full(bs_full.shape), _full(wo.shape),
                  _full((1, D_MODEL)), _full(wrh.shape), _full(wrl.shape)],
        out_specs=_route_out_specs(),
        out_shape=_ROUTE_OUT_SHAPES,
        scratch_shapes=[pltpu.VMEM((1, LANES), F32), pltpu.VMEM((TS, SG_WIDTH), BF16)],
        compiler_params=_cparams(),
        name="odd_mixer_route",
    )(h, g, win, vn, ws, bs_full, wo, gffn, wrh, wrl)


def _row_copy(src, si, dst, di, sem):
    return pltpu.make_async_copy(src.at[pl.ds(si, 1)], dst.at[pl.ds(di, 1)], sem)


def _dispatch_kernel(dest_ref, cnt_ref, pad_ref, pst_ref, nblk_ref, xn_ref, xs_hbm, zblk, sem, zsem):
    i = pl.program_id(0)

    def issue(r, _):
        t = i * TD + r
        _row_copy(xn_ref, r, xs_hbm, dest_ref[t], sem).start()
        _row_copy(xn_ref, r, xs_hbm, dest_ref[TOKENS + t], sem).start()
        return 0

    lax.fori_loop(0, TD, issue, 0, unroll=8)

    @pl.when(i == pl.num_programs(0) - 1)
    def _():
        zblk[...] = jnp.zeros_like(zblk)

        def per_expert(e, _):
            base = pst_ref[e]
            lax.fori_loop(cnt_ref[e], pad_ref[e],
                          lambda r, c: (_row_copy(zblk, 0, xs_hbm, base + r, zsem).start(), c)[1], 0)
            lax.fori_loop(cnt_ref[e], pad_ref[e],
                          lambda r, c: (_row_copy(zblk, 0, xs_hbm, base + r, zsem).wait(), c)[1], 0)
            return 0

        lax.fori_loop(0, N_EXPERTS, per_expert, 0)

        def tail_copy(b):
            return pltpu.make_async_copy(zblk, xs_hbm.at[pl.ds(pl.multiple_of(b * BM, BM), BM)], zsem)

        lax.fori_loop(nblk_ref[0], N_BLK, lambda b, c: (tail_copy(b).start(), c)[1], 0)
        lax.fori_loop(nblk_ref[0], N_BLK, lambda b, c: (tail_copy(b).wait(), c)[1], 0)

    def drain(r, _):
        t = i * TD + r
        _row_copy(xn_ref, r, xs_hbm, dest_ref[t], sem).wait()
        _row_copy(xn_ref, r, xs_hbm, dest_ref[TOKENS + t], sem).wait()
        return 0

    lax.fori_loop(0, TD, drain, 0, unroll=8)


def _dispatch(dest, cnt, padded, pstart, nblk, xn):
    return pl.pallas_call(
        _dispatch_kernel,
        grid_spec=pltpu.PrefetchScalarGridSpec(
            num_scalar_prefetch=5,
            grid=(TOKENS // TD,),
            in_specs=[pl.BlockSpec((TD, D_MODEL), lambda i, *_: (i, 0))],
            out_specs=pl.BlockSpec(memory_space=pl.ANY),
            scratch_shapes=[pltpu.VMEM((BM, D_MODEL), F32), pltpu.SemaphoreType.DMA, pltpu.SemaphoreType.DMA],
        ),
        out_shape=jax.ShapeDtypeStruct((P_ROWS, D_MODEL), F32),
        compiler_params=_cparams(),
        name="moe_dispatch",
    )(dest, cnt, padded, pstart, nblk, xn)


def _gmm_kernel(pst_ref, pad_ref, nblk_ref, wg_ref, wu_ref, wd_ref, xs_hbm, yb_hbm,
                wgb, wub, wdb, xbuf, ybuf, xsem, ysem, zsem):
    e = pl.program_id(0)
    nblk = nblk_ref[0]
    g0 = pst_ref[e] >> BM_SHIFT
    nb = pad_ref[e] >> BM_SHIFT

    def rows(g):
        return pl.ds(pl.multiple_of(g * BM, BM), BM)

    def x_copy(g, slot):
        return pltpu.make_async_copy(xs_hbm.at[rows(g)], xbuf.at[slot], xsem.at[slot])

    def y_copy(g, slot):
        return pltpu.make_async_copy(ybuf.at[slot], yb_hbm.at[rows(g)], ysem.at[slot])

    @pl.when(e == 0)
    def _():
        x_copy(0, 0).start()

    @pl.when(nb > 0)
    def _():
        wgb[...] = wg_ref[0].astype(BF16)
        wub[...] = wu_ref[0].astype(BF16)
        wdb[...] = wd_ref[0].astype(BF16)

        def block(g, c):
            slot = g & 1

            @pl.when(g + 1 < nblk)
            def _():
                x_copy(g + 1, 1 - slot).start()

            x_copy(g, slot).wait()

            @pl.when(g >= 2)
            def _():
                y_copy(g - 2, slot).wait()

            x = xbuf[slot].astype(BF16)
            gate = jnp.dot(x, wgb[...], preferred_element_type=F32)
            up = jnp.dot(x, wub[...], preferred_element_type=F32)
            mid = (gate * jax.nn.sigmoid(gate) * up).astype(BF16)
            ybuf[slot] = jnp.dot(mid, wdb[...], preferred_element_type=F32)
            y_copy(g, slot).start()
            return c

        lax.fori_loop(g0, g0 + nb, block, 0)

    @pl.when(e == pl.num_programs(0) - 1)
    def _():
        @pl.when(nblk >= 2)
        def _():
            y_copy(nblk - 2, nblk & 1).wait()

        y_copy(nblk - 1, (nblk - 1) & 1).wait()
        xbuf[0] = jnp.zeros((BM, D_MODEL), F32)

        def tail_copy(g):
            return pltpu.make_async_copy(xbuf.at[0], yb_hbm.at[rows(g)], zsem)

        lax.fori_loop(nblk, N_BLK, lambda g, c: (tail_copy(g).start(), c)[1], 0)
        lax.fori_loop(nblk, N_BLK, lambda g, c: (tail_copy(g).wait(), c)[1], 0)


def _gmm(pstart, padded, nblk, xs, w_gate, w_up, w_down, layer):
    wsel = lambda e, *_: (layer * N_EXPERTS + e, 0, 0)
    hbm = pl.BlockSpec(memory_space=pl.ANY)
    return pl.pallas_call(
        _gmm_kernel,
        grid_spec=pltpu.PrefetchScalarGridSpec(
            num_scalar_prefetch=3,
            grid=(N_EXPERTS,),
            in_specs=[pl.BlockSpec((1, D_MODEL, D_EXPERT), wsel),
                      pl.BlockSpec((1, D_MODEL, D_EXPERT), wsel),
                      pl.BlockSpec((1, D_EXPERT, D_MODEL), wsel),
                      hbm],
            out_specs=hbm,
            scratch_shapes=[pltpu.VMEM((D_MODEL, D_EXPERT), BF16), pltpu.VMEM((D_MODEL, D_EXPERT), BF16),
                            pltpu.VMEM((D_EXPERT, D_MODEL), BF16),
                            pltpu.VMEM((2, BM, D_MODEL), F32), pltpu.VMEM((2, BM, D_MODEL), F32),
                            pltpu.SemaphoreType.DMA((2,)), pltpu.SemaphoreType.DMA((2,)),
                            pltpu.SemaphoreType.DMA],
        ),
        out_shape=jax.ShapeDtypeStruct((P_ROWS, D_MODEL), F32),
        compiler_params=_cparams(),
        name="moe_grouped_matmul",
    )(pstart, padded, nblk, w_gate, w_up, w_down, xs)


def _combine_ple_kernel(dest_ref, h1_ref, route_ref, p_ref, gple_ref, wgate_ref, wproj_ref, fin_ref, yb_hbm,
                        out_ref, buf0, buf1, sem, *, final):
    i = pl.program_id(0)
    n = pl.num_programs(0)
    bufs = (buf0, buf1)

    def copies(tile, r, slot):
        t = tile * TS + r
        return (_row_copy(yb_hbm, dest_ref[t], bufs[slot], r, sem.at[slot]),
                _row_copy(yb_hbm, dest_ref[TOKENS + t], bufs[slot], TS + r, sem.at[slot]))

    def issue_loop(tile, slot):
        def body(r, c):
            for cp in copies(tile, r, slot):
                cp.start()
            return c

        lax.fori_loop(0, TS, body, 0, unroll=8)

    def issue_unrolled(tile, slot):
        for r in range(TS):
            for cp in copies(tile, r, slot):
                cp.start()

    def wait_loop(tile, slot):
        def body(r, c):
            for cp in copies(tile, r, slot):
                cp.wait()
            return c

        lax.fori_loop(0, TS, body, 0, unroll=8)

    def compute(half, slot):
        rows = slice(half * TS, (half + 1) * TS)
        route = route_ref[rows, :]
        w1 = route[:, 2:3]
        w2 = route[:, 3:4]
        h2 = h1_ref[rows, :] + (w1 * bufs[slot][0:TS, :] + w2 * bufs[slot][TS:2 * TS, :])
        gate = jax.nn.sigmoid(jnp.dot(_rms(h2, gple_ref[...]).astype(BF16), wgate_ref[...],
                                      preferred_element_type=F32))
        out = h2 + gate * jnp.dot(p_ref[rows, :].astype(BF16), wproj_ref[...], preferred_element_type=F32)
        if final:
            out = _rms(out, fin_ref[...])
        out_ref[rows, :] = out

    @pl.when(i == 0)
    def _():
        issue_loop(0, 0)

    first = 2 * i
    nxt = jnp.where(i + 1 < n, first + 2, 0)
    wait_loop(first, 0)
    issue_unrolled(first + 1, 1)
    compute(0, 0)
    wait_loop(first + 1, 1)
    issue_unrolled(nxt, 0)
    compute(1, 1)

    @pl.when(i == n - 1)
    def _():
        wait_loop(0, 0)


def _combine_ple(dest, h1, route, p_all, layer, gple, wgate, wproj, fin, yb, final):
    nt = TOKENS // (2 * TS)
    tok = lambda w: pl.BlockSpec((2 * TS, w), lambda i, *_: (i, 0))
    full = lambda shape: pl.BlockSpec(shape, lambda i, *_: (0,) * len(shape))
    return pl.pallas_call(
        functools.partial(_combine_ple_kernel, final=final),
        grid_spec=pltpu.PrefetchScalarGridSpec(
            num_scalar_prefetch=1,
            grid=(nt,),
            in_specs=[tok(D_MODEL), tok(LANES),
                      pl.BlockSpec((2 * TS, PLE_DIM), lambda i, *_: (layer * nt + i, 0)),
                      full((1, D_MODEL)), full(wgate.shape), full(wproj.shape), full((1, D_MODEL)),
                      pl.BlockSpec(memory_space=pl.ANY)],
            out_specs=tok(D_MODEL),
            scratch_shapes=[pltpu.VMEM((2 * TS, D_MODEL), F32), pltpu.VMEM((2 * TS, D_MODEL), F32),
                            pltpu.SemaphoreType.DMA((2,))],
        ),
        out_shape=jax.ShapeDtypeStruct((TOKENS, D_MODEL), F32),
        compiler_params=_cparams(),
        name="moe_combine_ple",
    )(dest, h1, route, p_all, gple, wgate, wproj, fin, yb)


def _plan_kernel(cnt_ref, lane_ref, dest_ref, cnt_out, pad_out, pst_out, nblk_out):
    def ranges(e, start):
        c = cnt_ref[ROUTE_LO + e]
        padded = ((c + (BM - 1)) >> BM_SHIFT) << BM_SHIFT
        cnt_out[e] = c
        pad_out[e] = padded
        pst_out[e] = start
        return start + padded

    total = lax.fori_loop(0, N_EXPERTS, ranges, jnp.int32(0))
    nblk_out[0] = total >> BM_SHIFT

    parts = TS // LANES
    for slot in range(2):
        e = lane_ref[:, slot * parts:(slot + 1) * parts, :].astype(jnp.int32)
        rank = lane_ref[:, (2 + slot) * parts:(3 + slot) * parts, :].astype(jnp.int32)
        start = jnp.zeros_like(e)
        for ex in range(N_EXPERTS):
            start = jnp.where(e == ex, pst_out[ex], start)
        dest_ref[slot] = start + rank


def _plan(cnt, route_lanes):
    smem = lambda: pl.BlockSpec(memory_space=pltpu.SMEM)
    nt = TOKENS // TS
    parts = TS // LANES
    i32 = lambda n: jax.ShapeDtypeStruct((n,), jnp.int32)
    dest, counts, padded, pstart, nblk = pl.pallas_call(
        _plan_kernel,
        grid_spec=pltpu.PrefetchScalarGridSpec(
            num_scalar_prefetch=1,
            grid=(1,),
            in_specs=[pl.BlockSpec(route_lanes.shape, lambda i, *_: (0, 0, 0))],
            out_specs=[pl.BlockSpec((2, nt, parts, LANES), lambda i, *_: (0, 0, 0, 0)),
                       smem(), smem(), smem(), smem()],
        ),
        out_shape=[jax.ShapeDtypeStruct((2, nt, parts, LANES), jnp.int32),
                   i32(N_EXPERTS), i32(N_EXPERTS), i32(N_EXPERTS), i32(1)],
        compiler_params=_cparams(),
        name="moe_plan",
    )(cnt.reshape(LANES), route_lanes)
    return dest.reshape(ASSIGN), counts, padded, pstart, nblk


def _router_weights(w_rg, w_re):
    w = jnp.zeros((D_MODEL, LANES), F32).at[:, :N_GROUPS].set(w_rg).at[:, ROUTE_LO:ROUTE_LO + N_EXPERTS].set(w_re)
    hi = w.astype(BF16)
    lo = (w - hi.astype(F32)).astype(BF16)
    return hi, lo


def _even_weights(w_in, w_q_up, w_kv_up):
    base = 3 * CONV_CH + Q_RANK + KV_RANK
    kpe_cols = jnp.zeros((D_MODEL, HEAD_PAD), F32).at[:, ROPE_LO:ROPE_LO + QK_ROPE].set(w_in[:, base:])
    win = jnp.concatenate([w_in[:, :base], kpe_cols], axis=1).astype(BF16)
    dqk = QK_NOPE + QK_ROPE
    wq = jnp.pad(w_q_up.reshape(Q_RANK, MLA_HEADS, dqk), ((0, 0), (0, 0), (0, HEAD_PAD - dqk)))
    wq = wq.reshape(Q_RANK, MLA_HEADS * HEAD_PAD).astype(BF16)
    wkv = w_kv_up.reshape(KV_RANK, MLA_HEADS, QK_NOPE + V_DIM)
    wk = jnp.pad(wkv[:, :, :QK_NOPE], ((0, 0), (0, 0), (0, HEAD_PAD - QK_NOPE))).reshape(KV_RANK, MLA_HEADS * HEAD_PAD)
    wv = wkv[:, :, QK_NOPE:].reshape(KV_RANK, MLA_HEADS * V_DIM)
    return win, wq, jnp.concatenate([wk, wv], axis=1).astype(BF16)


def kernel(x, p, positions, norm_mix, norm_ffn, w_in_e, conv_w, q_norm, w_q_up, kv_norm, w_kv_up, w_out_e,
           w_in_o, v_norm, w_s, b_s, w_out_o, w_router_group, w_router_expert, w_gate, w_up, w_down,
           norm_ple, w_ple_gate, w_ple_proj, final_norm):
    h = x.reshape(TOKENS, D_MODEL)
    p_all = p.reshape(DEPTH * TOKENS, PLE_DIM)
    wg_all = w_gate.reshape(DEPTH * N_EXPERTS, D_MODEL, D_EXPERT)
    wu_all = w_up.reshape(DEPTH * N_EXPERTS, D_MODEL, D_EXPERT)
    wd_all = w_down.reshape(DEPTH * N_EXPERTS, D_EXPERT, D_MODEL)
    ctab, stab = _rope_tables(positions)
    row = lambda a: a.reshape(1, -1)

    for i in range(DEPTH):
        j = i // 2
        wrh, wrl = _router_weights(w_router_group[i], w_router_expert[i])
        if i % 2 == 0:
            win, wq, wkv = _even_weights(w_in_e[j], w_q_up[j], w_kv_up[j])
            ya, q, k, v = _even_front(h, row(norm_mix[i]), win, conv_w[j], row(q_norm[j]), wq,
                                      row(kv_norm[j]), wkv, ctab, stab)
            yb = _attention(q, k, v)
            h1, xn, route, route_lanes, cnt = _even_out(ya, yb, w_out_e[j].astype(BF16), h, row(norm_ffn[i]),
                                                        wrh, wrl)
        else:
            bs_full = jnp.repeat(b_s[j].T, SG_WIDTH // SG_GROUPS, axis=1)
            h1, xn, route, route_lanes, cnt = _odd_layer(h, row(norm_mix[i]), w_in_o[j].astype(BF16),
                                                         row(v_norm[j]), w_s[j], bs_full, w_out_o[j].astype(BF16),
                                                         row(norm_ffn[i]), wrh, wrl)
        dest, counts, padded, pstart, nblk = _plan(cnt, route_lanes)
        xs = _dispatch(dest, counts, padded, pstart, nblk, xn)
        ybuf = _gmm(pstart, padded, nblk, xs, wg_all, wu_all, wd_all, i)
        h = _combine_ple(dest, h1, route, p_all, i, row(norm_ple[i]), w_ple_gate[i].astype(BF16),
                         w_ple_proj[i].astype(BF16), row(final_norm), ybuf, final=(i == DEPTH - 1))
    return h.reshape(BATCH, SEQ, D_MODEL)
```

```python
import functools
import math

import jax
import jax.numpy as jnp
from jax import lax
from jax.experimental import pallas as pl
from jax.experimental.pallas import tpu as pltpu

F32 = jnp.float32
BF16 = jnp.bfloat16

D_MODEL = 1024
BATCH = 4
SEQ = 4096
DEPTH = 4
TOKENS = BATCH * SEQ

CONV_CH = 512
MLA_HEADS = 8
QK_NOPE = 64
QK_ROPE = 32
V_DIM = 64
Q_RANK = 384
KV_RANK = 256
ROPE_THETA = 10000.0
SG_WIDTH = 1024
SG_GROUPS = 8
SG_CHUNK = 128
N_GROUPS = 4
EXP_PER_GROUP = 8
N_EXPERTS = N_GROUPS * EXP_PER_GROUP
D_EXPERT = 512
PLE_DIM = 256
NORM_EPS = 1e-6
NEG_INF = -1e30

LANES = 128
HEAD_PAD = LANES
ROPE_LO = QK_NOPE
ROPE_HALF = QK_ROPE // 2
EVEN_IN_PAD = 3 * CONV_CH + Q_RANK + KV_RANK + HEAD_PAD

TS = 256
TQ = 1024
BM = 256
BM_SHIFT = BM.bit_length() - 1
assert 1 << BM_SHIFT == BM
TD = 1024
ASSIGN = 2 * TOKENS
P_ROWS = ASSIGN + N_EXPERTS * BM
N_BLK = P_ROWS // BM
ROUTE_LO = N_GROUPS

VMEM_LIMIT = 48 * 1024 * 1024

Q_SCALE = math.log2(math.e) / math.sqrt(QK_NOPE + QK_ROPE)


def _cparams(n_axes=1):
    return pltpu.CompilerParams(dimension_semantics=("arbitrary",) * n_axes,
                                vmem_limit_bytes=VMEM_LIMIT)


def _rms(x, g):
    return x * lax.rsqrt(jnp.mean(x * x, axis=-1, keepdims=True) + NORM_EPS) * g


def _full(shape):
    return pl.BlockSpec(shape, lambda *_: (0,) * len(shape))


def _rope_table_kernel(pos_ref, freq_ref, c_ref, s_ref):
    ang = pos_ref[...].astype(F32) * freq_ref[...]
    lane = lax.broadcasted_iota(jnp.int32, ang.shape, 1)
    cosv = jnp.cos(ang)
    sinv = jnp.sin(ang)
    in_rope = (lane >= ROPE_LO) & (lane < ROPE_LO + QK_ROPE)
    first_half = lane < ROPE_LO + ROPE_HALF
    c_ref[...] = jnp.where(lane < ROPE_LO, 1.0, jnp.where(in_rope, cosv, 0.0))
    s_ref[...] = jnp.where(in_rope, jnp.where(first_half, -sinv, sinv), 0.0)


def _rope_tables(positions):
    inv_freq = 1.0 / (ROPE_THETA ** (jnp.arange(0, QK_ROPE, 2, dtype=F32) / QK_ROPE))
    freq = jnp.zeros((LANES,), F32).at[ROPE_LO:ROPE_LO + QK_ROPE].set(jnp.tile(inv_freq, 2))
    pos = positions.reshape(TOKENS, 1)
    return pl.pallas_call(
        _rope_table_kernel,
        grid=(TOKENS // TS,),
        in_specs=[pl.BlockSpec((TS, 1), lambda i: (i, 0)), _full((1, LANES))],
        out_specs=[pl.BlockSpec((TS, LANES), lambda i: (i, 0))] * 2,
        out_shape=[jax.ShapeDtypeStruct((TOKENS, LANES), F32)] * 2,
        compiler_params=_cparams(),
        name="rope_tables",
    )(pos, freq.reshape(1, LANES))


def _rope(x, c, s, first_half):
    partner = jnp.where(first_half, pltpu.roll(x, LANES - ROPE_HALF, 1), pltpu.roll(x, ROPE_HALF, 1))
    return x * c + partner * s


def _even_front_kernel(h_ref, g_ref, win_ref, cw_ref, qn_ref, wq_ref, kvn_ref, wkv_ref, c_ref, s_ref,
                       ya_ref, q_ref, k_ref, v_ref, zbuf):
    i = pl.program_id(0)
    hn = _rms(h_ref[...], g_ref[...]).astype(BF16)
    proj = jnp.dot(hn, win_ref[...], preferred_element_type=F32)
    xc = proj[:, 0:CONV_CH]
    gb = proj[:, CONV_CH:2 * CONV_CH]
    gc = proj[:, 2 * CONV_CH:3 * CONV_CH]
    o = 3 * CONV_CH
    cq = proj[:, o:o + Q_RANK]
    ckv = proj[:, o + Q_RANK:o + Q_RANK + KV_RANK]
    kpe = proj[:, o + Q_RANK + KV_RANK:]

    @pl.when(i % (SEQ // TS) == 0)
    def _():
        zbuf[0:8, :] = jnp.zeros((8, CONV_CH), F32)

    z = gc * xc
    zbuf[8:8 + TS, :] = z
    z1 = zbuf[7:7 + TS, :]
    z2 = zbuf[6:6 + TS, :]
    cw = cw_ref[...]
    conv = cw[0:1, :] * z2 + cw[1:2, :] * z1 + cw[2:3, :] * z
    ya_ref[...] = (gb * conv).astype(BF16)
    zbuf[0:8, :] = zbuf[TS:TS + 8, :]

    c = c_ref[...]
    s = s_ref[...]
    lane = lax.broadcasted_iota(jnp.int32, (TS, LANES), 1)
    first_half = lane < ROPE_LO + ROPE_HALF

    cqn = _rms(cq, qn_ref[...]).astype(BF16)
    q = jnp.dot(cqn, wq_ref[...], preferred_element_type=F32)
    for hd in range(MLA_HEADS):
        sl = slice(hd * HEAD_PAD, (hd + 1) * HEAD_PAD)
        q_ref[:, sl] = (_rope(q[:, sl], c, s, first_half) * Q_SCALE).astype(BF16)

    ckvn = _rms(ckv, kvn_ref[...]).astype(BF16)
    kv = jnp.dot(ckvn, wkv_ref[...], preferred_element_type=F32)
    kper = _rope(kpe, c, s, first_half)
    for hd in range(MLA_HEADS):
        sl = slice(hd * HEAD_PAD, (hd + 1) * HEAD_PAD)
        k_ref[:, sl] = (kv[:, sl] + kper).astype(BF16)
    v_ref[...] = kv[:, MLA_HEADS * HEAD_PAD:].astype(BF16)


def _even_front(h, g, win, cw, qn, wq, kvn, wkv, ctab, stab):
    tok = lambda w: pl.BlockSpec((TS, w), lambda i: (i, 0))
    return pl.pallas_call(
        _even_front_kernel,
        grid=(TOKENS // TS,),
        in_specs=[tok(D_MODEL), _full((1, D_MODEL)), _full(win.shape), _full(cw.shape),
                  _full((1, Q_RANK)), _full(wq.shape), _full((1, KV_RANK)), _full(wkv.shape),
                  tok(LANES), tok(LANES)],
        out_specs=[tok(CONV_CH), tok(MLA_HEADS * HEAD_PAD), tok(MLA_HEADS * HEAD_PAD), tok(MLA_HEADS * V_DIM)],
        out_shape=[jax.ShapeDtypeStruct((TOKENS, CONV_CH), BF16),
                   jax.ShapeDtypeStruct((TOKENS, MLA_HEADS * HEAD_PAD), BF16),
                   jax.ShapeDtypeStruct((TOKENS, MLA_HEADS * HEAD_PAD), BF16),
                   jax.ShapeDtypeStruct((TOKENS, MLA_HEADS * V_DIM), BF16)],
        scratch_shapes=[pltpu.VMEM((TS + 8, CONV_CH), F32)],
        compiler_params=_cparams(),
        name="even_front",
    )(h, g, win, cw, qn, wq, kvn, wkv, ctab, stab)


def _attn_kernel(q_ref, k_ref, v_ref, o_ref):
    i = pl.program_id(2)
    row = lax.broadcasted_iota(jnp.int32, (TQ, TQ), 0)
    col = lax.broadcasted_iota(jnp.int32, (TQ, TQ), 1)
    causal = col <= row
    nt = (((1,), (1,)), ((), ()))
    for hh in range(2):
        hs = slice(hh * HEAD_PAD, (hh + 1) * HEAD_PAD)
        q = q_ref[:, hs]

        def step(j, carry, masked):
            m, l, acc = carry
            kb = k_ref[pl.ds(pl.multiple_of(j * TQ, TQ), TQ), hs]
            vb = v_ref[pl.ds(pl.multiple_of(j * TQ, TQ), TQ), :]
            s = lax.dot_general(q, kb, nt, preferred_element_type=F32)
            if masked:
                s = jnp.where(causal, s, NEG_INF)
            m_new = jnp.maximum(m, jnp.max(s, axis=-1, keepdims=True))
            alpha = jnp.exp2(m - m_new)
            p = jnp.exp2(s - m_new)
            l = alpha * l + jnp.sum(p, axis=-1, keepdims=True)
            acc = alpha * acc + jnp.dot(p.astype(BF16), vb, preferred_element_type=F32)
            return m_new, l, acc

        init = (jnp.full((TQ, 1), NEG_INF, F32), jnp.zeros((TQ, 1), F32), jnp.zeros((TQ, 2 * V_DIM), F32))
        carry = lax.fori_loop(0, i, lambda j, cr: step(j, cr, False), init)
        m, l, acc = step(i, carry, True)
        vs = slice(hh * V_DIM, (hh + 1) * V_DIM)
        o_ref[:, vs] = (acc / l)[:, vs].astype(BF16)


def _attention(q, k, v):
    nq = SEQ // TQ
    pairs = MLA_HEADS // 2
    return pl.pallas_call(
        _attn_kernel,
        grid=(BATCH, pairs, nq),
        in_specs=[pl.BlockSpec((TQ, 2 * HEAD_PAD), lambda b, p, i: (b * nq + i, p)),
                  pl.BlockSpec((SEQ, 2 * HEAD_PAD), lambda b, p, i: (b, p)),
                  pl.BlockSpec((SEQ, 2 * V_DIM), lambda b, p, i: (b, p))],
        out_specs=pl.BlockSpec((TQ, 2 * V_DIM), lambda b, p, i: (b * nq + i, p)),
        out_shape=jax.ShapeDtypeStruct((TOKENS, MLA_HEADS * V_DIM), BF16),
        compiler_params=_cparams(3),
        name="mla_attention",
    )(q, k, v)


def _route(h1, gffn, wr_hi, wr_lo, carry, xn_ref, route_ref, lane_ref, cnt_ref):
    xn = _rms(h1, gffn)
    xn_ref[...] = xn
    x_hi = xn.astype(BF16)
    x_lo = (xn - x_hi.astype(F32)).astype(BF16)
    logits = (jnp.dot(x_hi, wr_hi, preferred_element_type=F32)
              + jnp.dot(x_lo, wr_hi, preferred_element_type=F32)
              + jnp.dot(x_hi, wr_lo, preferred_element_type=F32))
    lane = lax.broadcasted_iota(jnp.int32, logits.shape, 1).astype(F32)
    big = float(LANES)

    is_g = lane < N_GROUPS
    gl = jnp.where(is_g, logits, NEG_INF)
    gmax = jnp.max(gl, axis=-1, keepdims=True)
    gidx = jnp.min(jnp.where(gl == gmax, lane, big), axis=-1, keepdims=True)
    gsum = jnp.sum(jnp.where(is_g, jnp.exp(gl - gmax), 0.0), axis=-1, keepdims=True)
    g_w = 1.0 / gsum

    lo = ROUTE_LO + gidx * EXP_PER_GROUP
    el = jnp.where((lane >= lo) & (lane < lo + EXP_PER_GROUP), logits, NEG_INF)
    m1 = jnp.max(el, axis=-1, keepdims=True)
    i1 = jnp.min(jnp.where(el == m1, lane, big), axis=-1, keepdims=True)
    el2 = jnp.where(lane == i1, NEG_INF, el)
    m2 = jnp.max(el2, axis=-1, keepdims=True)
    i2 = jnp.min(jnp.where(el2 == m2, lane, big), axis=-1, keepdims=True)
    t = jnp.exp(m2 - m1)
    w1 = g_w / (1.0 + t)
    w2 = g_w * t / (1.0 + t)

    oh1 = lane == i1
    oh2 = lane == i2
    oh = jnp.where(oh1 | oh2, 1.0, 0.0)
    r = lax.broadcasted_iota(jnp.int32, (TS, TS), 0)
    cc = lax.broadcasted_iota(jnp.int32, (TS, TS), 1)
    tri = jnp.where(cc < r, 1.0, 0.0).astype(BF16)
    before = jnp.dot(tri, oh.astype(BF16), preferred_element_type=F32) + carry[...]
    rank1 = jnp.sum(jnp.where(oh1, before, 0.0), axis=-1, keepdims=True)
    rank2 = jnp.sum(jnp.where(oh2, before, 0.0), axis=-1, keepdims=True)
    carry[...] = carry[...] + jnp.sum(oh, axis=0, keepdims=True)
    cnt_ref[...] = carry[...].astype(jnp.int32)

    vals = (i1 - ROUTE_LO, i2 - ROUTE_LO, w1, w2, rank1, rank2)
    out = jnp.zeros_like(logits)
    for n, val in enumerate(vals):
        out = jnp.where(lane == float(n), val, out)
    route_ref[...] = out
    out_t = out.T
    rows = [out_t[n:n + 1, part * LANES:(part + 1) * LANES] for n in (0, 1, 4, 5) for part in range(TS // LANES)]
    lane_ref[0] = jnp.concatenate(rows, axis=0)


_ROUTE_OUT_SHAPES = [jax.ShapeDtypeStruct((TOKENS, D_MODEL), F32),
                     jax.ShapeDtypeStruct((TOKENS, D_MODEL), F32),
                     jax.ShapeDtypeStruct((TOKENS, LANES), F32),
                     jax.ShapeDtypeStruct((TOKENS // TS, 4 * (TS // LANES), LANES), F32),
                     jax.ShapeDtypeStruct((1, LANES), jnp.int32)]


def _route_out_specs():
    tok = lambda w: pl.BlockSpec((TS, w), lambda i: (i, 0))
    return [tok(D_MODEL), tok(D_MODEL), tok(LANES),
            pl.BlockSpec((1, 4 * (TS // LANES), LANES), lambda i: (i, 0, 0)), _full((1, LANES))]


def _even_out_kernel(ya_ref, yb_ref, wo_ref, h_ref, gffn_ref, wrh_ref, wrl_ref,
                     h1_ref, xn_ref, route_ref, lane_ref, cnt_ref, carry):
    @pl.when(pl.program_id(0) == 0)
    def _():
        carry[...] = jnp.zeros_like(carry)

    mix = (jnp.dot(ya_ref[...], wo_ref[0:CONV_CH, :], preferred_element_type=F32)
           + jnp.dot(yb_ref[...], wo_ref[CONV_CH:, :], preferred_element_type=F32))
    h1 = h_ref[...] + mix
    h1_ref[...] = h1
    _route(h1, gffn_ref[...], wrh_ref[...], wrl_ref[...], carry, xn_ref, route_ref, lane_ref, cnt_ref)


def _even_out(ya, yb, wo, h, gffn, wrh, wrl):
    tok = lambda w: pl.BlockSpec((TS, w), lambda i: (i, 0))
    return pl.pallas_call(
        _even_out_kernel,
        grid=(TOKENS // TS,),
        in_specs=[tok(CONV_CH), tok(MLA_HEADS * V_DIM), _full(wo.shape), tok(D_MODEL),
                  _full((1, D_MODEL)), _full(wrh.shape), _full(wrl.shape)],
        out_specs=_route_out_specs(),
        out_shape=_ROUTE_OUT_SHAPES,
        scratch_shapes=[pltpu.VMEM((1, LANES), F32)],
        compiler_params=_cparams(),
        name="even_out_route",
    )(ya, yb, wo, h, gffn, wrh, wrl)


def _gelu_tanh(x):
    return 0.5 * x * (1.0 + jnp.tanh(math.sqrt(2.0 / math.pi) * (x + 0.044715 * (x * x * x))))


def _odd_kernel(h_ref, g_ref, win_ref, vn_ref, ws_ref, bs_ref, wo_ref, gffn_ref, wrh_ref, wrl_ref,
                h1_ref, xn_ref, route_ref, lane_ref, cnt_ref, carry, mixbuf):
    @pl.when(pl.program_id(0) == 0)
    def _():
        carry[...] = jnp.zeros_like(carry)

    h = h_ref[...]
    hn = _rms(h, g_ref[...]).astype(BF16)
    z = _gelu_tanh(jnp.dot(hn, win_ref[...], preferred_element_type=F32))
    u = z[:, :SG_WIDTH]
    v = _rms(z[:, SG_WIDTH:], vn_ref[...]).astype(BF16)
    r = lax.broadcasted_iota(jnp.int32, (SG_CHUNK, SG_CHUNK), 0)
    c = lax.broadcasted_iota(jnp.int32, (SG_CHUNK, SG_CHUNK), 1)
    n_chunks = TS // SG_CHUNK
    dg = SG_WIDTH // SG_GROUPS
    for gi in range(SG_GROUPS):
        gs = slice(gi * dg, (gi + 1) * dg)
        wtril = jnp.where(c <= r, ws_ref[gi], 0.0).astype(BF16)
        rhs = jnp.concatenate([v[ci * SG_CHUNK:(ci + 1) * SG_CHUNK, gs] for ci in range(n_chunks)], axis=1)
        sg = jnp.dot(wtril, rhs, preferred_element_type=F32)
        for ci in range(n_chunks):
            rs = slice(ci * SG_CHUNK, (ci + 1) * SG_CHUNK)
            gate = sg[:, ci * dg:(ci + 1) * dg] + bs_ref[:, gs]
            mixbuf[rs, gs] = (u[rs, gs] * gate).astype(BF16)
    h1 = h + jnp.dot(mixbuf[...], wo_ref[...], preferred_element_type=F32)
    h1_ref[...] = h1
    _route(h1, gffn_ref[...], wrh_ref[...], wrl_ref[...], carry, xn_ref, route_ref, lane_ref, cnt_ref)


def _odd_layer(h, g, win, vn, ws, bs_full, wo, gffn, wrh, wrl):
    tok = lambda w: pl.BlockSpec((TS, w), lambda i: (i, 0))
    return pl.pallas_call(
        _odd_kernel,
        grid=(TOKENS // TS,),
        in_specs=[tok(D_MODEL), _full((1, D_MODEL)), _full(win.shape), _full((1, SG_WIDTH)),
                  _full(ws.shape), _full(bs_full.shape), _full(wo.shape),
                  _full((1, D_MODEL)), _full(wrh.shape), _full(wrl.shape)],
        out_specs=_route_out_specs(),
        out_shape=_ROUTE_OUT_SHAPES,
        scratch_shapes=[pltpu.VMEM((1, LANES), F32), pltpu.VMEM((TS, SG_WIDTH), BF16)],
        compiler_params=_cparams(),
        name="odd_mixer_route",
    )(h, g, win, vn, ws, bs_full, wo, gffn, wrh, wrl)


def _row_copy(src, si, dst, di, sem):
    return pltpu.make_async_copy(src.at[pl.ds(si, 1)], dst.at[pl.ds(di, 1)], sem)


X_SLOTS = 4
GATHER_AHEAD = 2


def _gmm_kernel(tok_ref, pst_ref, pad_ref, nblk_ref, wg_ref, wu_ref, wd_ref, xn_hbm, yb_hbm,
                wgb, wub, wdb, x0, x1, x2, x3, y0, y1, gsem, ysem, zsem):
    e = pl.program_id(0)
    nblk = nblk_ref[0]
    xb = (x0, x1, x2, x3)
    yb = (y0, y1)
    g0 = pst_ref[e] >> BM_SHIFT
    nb = pad_ref[e] >> BM_SHIFT

    def rows(g):
        return pl.ds(pl.multiple_of(g * BM, BM), BM)

    def gather(g, r, k):
        return _row_copy(xn_hbm, tok_ref[g * BM + r], xb[k], r, gsem.at[k])

    def gather_loop(g, k, wait):
        def body(r, c):
            if wait:
                gather(g, r, k).wait()
            else:
                gather(g, r, k).start()
            return c

        lax.fori_loop(0, BM, body, 0, unroll=8)

    def y_copy(g, s):
        return pltpu.make_async_copy(yb[s], yb_hbm.at[rows(g)], ysem.at[s])

    @pl.when(e == 0)
    def _():
        for k in range(GATHER_AHEAD):
            gather_loop(jnp.minimum(k, nblk - 1), k, False)

    @pl.when(nb > 0)
    def _():
        wgb[...] = wg_ref[0].astype(BF16)
        wub[...] = wu_ref[0].astype(BF16)
        wdb[...] = wd_ref[0].astype(BF16)

        def block(g, c):
            ahead = jnp.minimum(g + GATHER_AHEAD, nblk - 1)

            def variant(k):
                def run():
                    gather_loop(g, k, True)

                    @pl.when(g >= 2)
                    def _():
                        y_copy(g - 2, k & 1).wait()

                    for r in range(BM):
                        gather(ahead, r, (k + GATHER_AHEAD) % X_SLOTS).start()
                    x = xb[k][...].astype(BF16)
                    gate = jnp.dot(x, wgb[...], preferred_element_type=F32)
                    up = jnp.dot(x, wub[...], preferred_element_type=F32)
                    mid = (gate * jax.nn.sigmoid(gate) * up).astype(BF16)
                    yb[k & 1][...] = jnp.dot(mid, wdb[...], preferred_element_type=F32)
                    y_copy(g, k & 1).start()

                return run

            lax.switch(g % X_SLOTS, [variant(k) for k in range(X_SLOTS)])
            return c

        lax.fori_loop(g0, g0 + nb, block, 0)

    @pl.when(e == pl.num_programs(0) - 1)
    def _():
        for s in range(2):
            @pl.when((nblk >= 2) & ((nblk & 1) == s))
            def _():
                y_copy(nblk - 2, s).wait()

            @pl.when(((nblk - 1) & 1) == s)
            def _():
                y_copy(nblk - 1, s).wait()

        for ahead in range(GATHER_AHEAD):
            for k in range(X_SLOTS):
                @pl.when((nblk + ahead) % X_SLOTS == k)
                def _():
                    gather_loop(nblk - 1, k, True)
        x0[...] = jnp.zeros((BM, D_MODEL), F32)

        def tail_copy(g):
            return pltpu.make_async_copy(x0, yb_hbm.at[rows(g)], zsem)

        lax.fori_loop(nblk, N_BLK, lambda g, c: (tail_copy(g).start(), c)[1], 0)
        lax.fori_loop(nblk, N_BLK, lambda g, c: (tail_copy(g).wait(), c)[1], 0)


def _gmm(buf_tok, pstart, padded, nblk, xn, w_gate, w_up, w_down, layer):
    wsel = lambda e, *_: (layer * N_EXPERTS + e, 0, 0)
    hbm = pl.BlockSpec(memory_space=pl.ANY)
    blk = lambda dt: pltpu.VMEM((BM, D_MODEL), dt)
    return pl.pallas_call(
        _gmm_kernel,
        grid_spec=pltpu.PrefetchScalarGridSpec(
            num_scalar_prefetch=4,
            grid=(N_EXPERTS,),
            in_specs=[pl.BlockSpec((1, D_MODEL, D_EXPERT), wsel),
                      pl.BlockSpec((1, D_MODEL, D_EXPERT), wsel),
                      pl.BlockSpec((1, D_EXPERT, D_MODEL), wsel),
                      hbm],
            out_specs=hbm,
            scratch_shapes=[pltpu.VMEM((D_MODEL, D_EXPERT), BF16), pltpu.VMEM((D_MODEL, D_EXPERT), BF16),
                            pltpu.VMEM((D_EXPERT, D_MODEL), BF16)]
                           + [blk(F32)] * (X_SLOTS + 2)
                           + [pltpu.SemaphoreType.DMA((X_SLOTS,)), pltpu.SemaphoreType.DMA((2,)),
                              pltpu.SemaphoreType.DMA],
        ),
        out_shape=jax.ShapeDtypeStruct((P_ROWS, D_MODEL), F32),
        compiler_params=_cparams(),
        name="moe_grouped_matmul",
    )(buf_tok, pstart, padded, nblk, w_gate, w_up, w_down, xn)


def _combine_ple_kernel(dest_ref, h1_ref, route_ref, p_ref, gple_ref, wgate_ref, wproj_ref, fin_ref, yb_hbm,
                        out_ref, buf0, buf1, buf2, buf3, sem, *, final):
    i = pl.program_id(0)
    n = pl.num_programs(0)
    bufs = (buf0, buf1, buf2, buf3)
    n_tiles = TOKENS // TS

    def copies(tile, r, slot):
        t = tile * TS + r
        return (_row_copy(yb_hbm, dest_ref[t], bufs[slot], r, sem.at[slot]),
                _row_copy(yb_hbm, dest_ref[TOKENS + t], bufs[slot], TS + r, sem.at[slot]))

    def issue_loop(tile, slot):
        def body(r, c):
            for cp in copies(tile, r, slot):
                cp.start()
            return c

        lax.fori_loop(0, TS, body, 0, unroll=8)

    def issue_unrolled(tile, slot):
        for r in range(TS):
            for cp in copies(tile, r, slot):
                cp.start()

    def wait_loop(tile, slot):
        def body(r, c):
            for cp in copies(tile, r, slot):
                cp.wait()
            return c

        lax.fori_loop(0, TS, body, 0, unroll=8)

    def compute(half, slot):
        rows = slice(half * TS, (half + 1) * TS)
        route = route_ref[rows, :]
        w1 = route[:, 2:3]
        w2 = route[:, 3:4]
        h2 = h1_ref[rows, :] + (w1 * bufs[slot][0:TS, :] + w2 * bufs[slot][TS:2 * TS, :])
        gate = jax.nn.sigmoid(jnp.dot(_rms(h2, gple_ref[...]).astype(BF16), wgate_ref[...],
                                      preferred_element_type=F32))
        out = h2 + gate * jnp.dot(p_ref[rows, :].astype(BF16), wproj_ref[...], preferred_element_type=F32)
        if final:
            out = _rms(out, fin_ref[...])
        out_ref[rows, :] = out

    @pl.when(i == 0)
    def _():
        for k in range(GATHER_AHEAD):
            issue_loop(k, k)

    for k in range(X_SLOTS):
        tile = X_SLOTS * i + k
        ahead = jnp.minimum(tile + GATHER_AHEAD, n_tiles - 1)
        wait_loop(tile, k)
        issue_unrolled(ahead, (k + GATHER_AHEAD) % X_SLOTS)
        compute(k, k)

    @pl.when(i == n - 1)
    def _():
        for k in range(GATHER_AHEAD):
            wait_loop(n_tiles - 1, (n_tiles + k) % X_SLOTS)


def _combine_ple(dest, h1, route, p_all, layer, gple, wgate, wproj, fin, yb, final):
    nt = TOKENS // (X_SLOTS * TS)
    tok = lambda w: pl.BlockSpec((X_SLOTS * TS, w), lambda i, *_: (i, 0))
    full = lambda shape: pl.BlockSpec(shape, lambda i, *_: (0,) * len(shape))
    return pl.pallas_call(
        functools.partial(_combine_ple_kernel, final=final),
        grid_spec=pltpu.PrefetchScalarGridSpec(
            num_scalar_prefetch=1,
            grid=(nt,),
            in_specs=[tok(D_MODEL), tok(LANES),
                      pl.BlockSpec((X_SLOTS * TS, PLE_DIM), lambda i, *_: (layer * nt + i, 0)),
                      full((1, D_MODEL)), full(wgate.shape), full(wproj.shape), full((1, D_MODEL)),
                      pl.BlockSpec(memory_space=pl.ANY)],
            out_specs=tok(D_MODEL),
            scratch_shapes=[pltpu.VMEM((2 * TS, D_MODEL), F32)] * X_SLOTS + [pltpu.SemaphoreType.DMA((X_SLOTS,))],
        ),
        out_shape=jax.ShapeDtypeStruct((TOKENS, D_MODEL), F32),
        compiler_params=_cparams(),
        name="moe_combine_ple",
    )(dest, h1, route, p_all, gple, wgate, wproj, fin, yb)


def _plan_kernel(cnt_ref, lane_ref, dest_ref, tok_out, pad_out, pst_out, nblk_out, dvm, dsm, sem):
    parts = TS // LANES
    nt = TOKENS // TS

    def ranges(e, start):
        c = cnt_ref[ROUTE_LO + e]
        padded = ((c + (BM - 1)) >> BM_SHIFT) << BM_SHIFT
        pad_out[e] = padded
        pst_out[e] = start

        def clear(r, k):
            tok_out[start + r] = 0
            return k

        lax.fori_loop(c, padded, clear, 0)
        return start + padded

    total = lax.fori_loop(0, N_EXPERTS, ranges, jnp.int32(0))
    nblk_out[0] = total >> BM_SHIFT

    def clear_tail(r, k):
        tok_out[r] = 0
        return k

    lax.fori_loop(total, P_ROWS, clear_tail, 0)

    dests = []
    for slot in range(2):
        e = lane_ref[:, slot * parts:(slot + 1) * parts, :].astype(jnp.int32)
        rank = lane_ref[:, (2 + slot) * parts:(3 + slot) * parts, :].astype(jnp.int32)
        start = jnp.zeros_like(e)
        for ex in range(N_EXPERTS):
            start = jnp.where(e == ex, pst_out[ex], start)
        dest_ref[slot] = start + rank
        dests.append(start + rank)

    fill = jnp.zeros((nt, 8 - 2 * parts, LANES), jnp.int32)
    dvm[...] = jnp.concatenate(dests + [fill], axis=1).reshape(nt * 8, LANES)
    stage = pltpu.make_async_copy(dvm, dsm, sem)
    stage.start()
    stage.wait()

    def per_tile(tile, k):
        for part in range(parts):
            row = tile * 8 + part
            t0 = tile * TS + part * LANES

            def per_lane(lane, k2):
                tok_out[dsm[row, lane]] = t0 + lane
                tok_out[dsm[row + parts, lane]] = t0 + lane
                return k2

            lax.fori_loop(0, LANES, per_lane, 0, unroll=16)
        return k

    lax.fori_loop(0, nt, per_tile, 0)


def _plan(cnt, route_lanes):
    smem = lambda: pl.BlockSpec(memory_space=pltpu.SMEM)
    nt = TOKENS // TS
    parts = TS // LANES
    i32 = lambda n: jax.ShapeDtypeStruct((n,), jnp.int32)
    dest, buf_tok, padded, pstart, nblk = pl.pallas_call(
        _plan_kernel,
        grid_spec=pltpu.PrefetchScalarGridSpec(
            num_scalar_prefetch=1,
            grid=(1,),
            in_specs=[pl.BlockSpec(route_lanes.shape, lambda i, *_: (0, 0, 0))],
            out_specs=[pl.BlockSpec((2, nt, parts, LANES), lambda i, *_: (0, 0, 0, 0)),
                       smem(), smem(), smem(), smem()],
            scratch_shapes=[pltpu.VMEM((nt * 8, LANES), jnp.int32), pltpu.SMEM((nt * 8, LANES), jnp.int32),
                            pltpu.SemaphoreType.DMA],
        ),
        out_shape=[jax.ShapeDtypeStruct((2, nt, parts, LANES), jnp.int32),
                   i32(P_ROWS), i32(N_EXPERTS), i32(N_EXPERTS), i32(1)],
        compiler_params=_cparams(),
        name="moe_plan",
    )(cnt.reshape(LANES), route_lanes)
    return dest.reshape(ASSIGN), buf_tok, padded, pstart, nblk


def _router_weights(w_rg, w_re):
    w = jnp.zeros((D_MODEL, LANES), F32).at[:, :N_GROUPS].set(w_rg).at[:, ROUTE_LO:ROUTE_LO + N_EXPERTS].set(w_re)
    hi = w.astype(BF16)
    lo = (w - hi.astype(F32)).astype(BF16)
    return hi, lo


def _even_weights(w_in, w_q_up, w_kv_up):
    base = 3 * CONV_CH + Q_RANK + KV_RANK
    kpe_cols = jnp.zeros((D_MODEL, HEAD_PAD), F32).at[:, ROPE_LO:ROPE_LO + QK_ROPE].set(w_in[:, base:])
    win = jnp.concatenate([w_in[:, :base], kpe_cols], axis=1).astype(BF16)
    dqk = QK_NOPE + QK_ROPE
    wq = jnp.pad(w_q_up.reshape(Q_RANK, MLA_HEADS, dqk), ((0, 0), (0, 0), (0, HEAD_PAD - dqk)))
    wq = wq.reshape(Q_RANK, MLA_HEADS * HEAD_PAD).astype(BF16)
    wkv = w_kv_up.reshape(KV_RANK, MLA_HEADS, QK_NOPE + V_DIM)
    wk = jnp.pad(wkv[:, :, :QK_NOPE], ((0, 0), (0, 0), (0, HEAD_PAD - QK_NOPE))).reshape(KV_RANK, MLA_HEADS * HEAD_PAD)
    wv = wkv[:, :, QK_NOPE:].reshape(KV_RANK, MLA_HEADS * V_DIM)
    return win, wq, jnp.concatenate([wk, wv], axis=1).astype(BF16)


def kernel(x, p, positions, norm_mix, norm_ffn, w_in_e, conv_w, q_norm, w_q_up, kv_norm, w_kv_up, w_out_e,
           w_in_o, v_norm, w_s, b_s, w_out_o, w_router_group, w_router_expert, w_gate, w_up, w_down,
           norm_ple, w_ple_gate, w_ple_proj, final_norm):
    h = x.reshape(TOKENS, D_MODEL)
    p_all = p.reshape(DEPTH * TOKENS, PLE_DIM)
    wg_all = w_gate.reshape(DEPTH * N_EXPERTS, D_MODEL, D_EXPERT)
    wu_all = w_up.reshape(DEPTH * N_EXPERTS, D_MODEL, D_EXPERT)
    wd_all = w_down.reshape(DEPTH * N_EXPERTS, D_EXPERT, D_MODEL)
    ctab, stab = _rope_tables(positions)
    row = lambda a: a.reshape(1, -1)

    for i in range(DEPTH):
        j = i // 2
        wrh, wrl = _router_weights(w_router_group[i], w_router_expert[i])
        if i % 2 == 0:
            win, wq, wkv = _even_weights(w_in_e[j], w_q_up[j], w_kv_up[j])
            ya, q, k, v = _even_front(h, row(norm_mix[i]), win, conv_w[j], row(q_norm[j]), wq,
                                      row(kv_norm[j]), wkv, ctab, stab)
            yb = _attention(q, k, v)
            h1, xn, route, route_lanes, cnt = _even_out(ya, yb, w_out_e[j].astype(BF16), h, row(norm_ffn[i]),
                                                        wrh, wrl)
        else:
            bs_full = jnp.repeat(b_s[j].T, SG_WIDTH // SG_GROUPS, axis=1)
            h1, xn, route, route_lanes, cnt = _odd_layer(h, row(norm_mix[i]), w_in_o[j].astype(BF16),
                                                         row(v_norm[j]), w_s[j], bs_full, w_out_o[j].astype(BF16),
                                                         row(norm_ffn[i]), wrh, wrl)
        dest, buf_tok, padded, pstart, nblk = _plan(cnt, route_lanes)
        ybuf = _gmm(buf_tok, pstart, padded, nblk, xn, wg_all, wu_all, wd_all, i)
        h = _combine_ple(dest, h1, route, p_all, i, row(norm_ple[i]), w_ple_gate[i].astype(BF16),
                         w_ple_proj[i].astype(BF16), row(final_norm), ybuf, final=(i == DEPTH - 1))
    return h.reshape(BATCH, SEQ, D_MODEL)
```

```python
import functools
import math

import jax
import jax.numpy as jnp
from jax import lax
from jax.experimental import pallas as pl
from jax.experimental.pallas import tpu as pltpu

F32 = jnp.float32
BF16 = jnp.bfloat16
I32 = jnp.int32

D_MODEL = 1024
BATCH = 4
SEQ = 4096
DEPTH = 4
TOKENS = BATCH * SEQ

CONV_CH = 512
MLA_HEADS = 8
QK_NOPE = 64
QK_ROPE = 32
V_DIM = 64
Q_RANK = 384
KV_RANK = 256
ROPE_THETA = 10000.0
SG_WIDTH = 1024
SG_GROUPS = 8
SG_CHUNK = 128
N_GROUPS = 4
EXP_PER_GROUP = 8
N_EXPERTS = N_GROUPS * EXP_PER_GROUP
D_EXPERT = 512
PLE_DIM = 256
NORM_EPS = 1e-6
NEG_INF = -1e30

LANES = 128
HEAD_PAD = LANES
ROPE_LO = QK_NOPE
ROPE_HALF = QK_ROPE // 2

TS = 256
PARTS = TS // LANES
TQ = 1024
BM = 256
BM_SHIFT = BM.bit_length() - 1
assert 1 << BM_SHIFT == BM and TS == BM
N_TILES = TOKENS // TS
N_CHUNKS = 2 * TOKENS // BM + N_EXPERTS
NJ = TOKENS // BM + 1
XS_ROWS = (N_CHUNKS + 2) * BM
DEST_HALF = TOKENS + TS
ROUTE_LO = N_GROUPS
STG_ROWS = 16
CNT_ROW = 4 * PARTS
X_SLOTS = 4
AHEAD = 2

VMEM_LIMIT = 48 * 1024 * 1024

Q_SCALE = math.log2(math.e) / math.sqrt(QK_NOPE + QK_ROPE)


def _cparams(n_axes=1):
    return pltpu.CompilerParams(dimension_semantics=("arbitrary",) * n_axes,
                                vmem_limit_bytes=VMEM_LIMIT)


def _rms(x, g):
    return x * lax.rsqrt(jnp.mean(x * x, axis=-1, keepdims=True) + NORM_EPS) * g


def _full(shape):
    return pl.BlockSpec(shape, lambda *_: (0,) * len(shape))


def _row_copy(src, si, dst, di, sem):
    return pltpu.make_async_copy(src.at[pl.ds(si, 1)], dst.at[pl.ds(di, 1)], sem)


def _chunk_rows(chunk):
    return pl.ds(pl.multiple_of(chunk * BM, BM), BM)


def _rope_table_kernel(pos_ref, freq_ref, c_ref, s_ref):
    ang = pos_ref[...].astype(F32) * freq_ref[...]
    lane = lax.broadcasted_iota(I32, ang.shape, 1)
    cosv = jnp.cos(ang)
    sinv = jnp.sin(ang)
    in_rope = (lane >= ROPE_LO) & (lane < ROPE_LO + QK_ROPE)
    first_half = lane < ROPE_LO + ROPE_HALF
    c_ref[...] = jnp.where(lane < ROPE_LO, 1.0, jnp.where(in_rope, cosv, 0.0))
    s_ref[...] = jnp.where(in_rope, jnp.where(first_half, -sinv, sinv), 0.0)


def _rope_tables(positions):
    inv_freq = 1.0 / (ROPE_THETA ** (jnp.arange(0, QK_ROPE, 2, dtype=F32) / QK_ROPE))
    freq = jnp.zeros((LANES,), F32).at[ROPE_LO:ROPE_LO + QK_ROPE].set(jnp.tile(inv_freq, 2))
    pos = positions.reshape(TOKENS, 1)
    return pl.pallas_call(
        _rope_table_kernel,
        grid=(TOKENS // TS,),
        in_specs=[pl.BlockSpec((TS, 1), lambda i: (i, 0)), _full((1, LANES))],
        out_specs=[pl.BlockSpec((TS, LANES), lambda i: (i, 0))] * 2,
        out_shape=[jax.ShapeDtypeStruct((TOKENS, LANES), F32)] * 2,
        compiler_params=_cparams(),
        name="rope_tables",
    )(pos, freq.reshape(1, LANES))


def _rope(x, c, s, first_half):
    partner = jnp.where(first_half, pltpu.roll(x, LANES - ROPE_HALF, 1), pltpu.roll(x, ROPE_HALF, 1))
    return x * c + partner * s


def _even_front_kernel(h_ref, g_ref, win_ref, cw_ref, qn_ref, wq_ref, kvn_ref, wkv_ref, c_ref, s_ref,
                       ya_ref, q_ref, k_ref, v_ref, zbuf):
    i = pl.program_id(0)
    hn = _rms(h_ref[...], g_ref[...]).astype(BF16)
    proj = jnp.dot(hn, win_ref[...], preferred_element_type=F32)
    xc = proj[:, 0:CONV_CH]
    gb = proj[:, CONV_CH:2 * CONV_CH]
    gc = proj[:, 2 * CONV_CH:3 * CONV_CH]
    o = 3 * CONV_CH
    cq = proj[:, o:o + Q_RANK]
    ckv = proj[:, o + Q_RANK:o + Q_RANK + KV_RANK]
    kpe = proj[:, o + Q_RANK + KV_RANK:]

    @pl.when(i % (SEQ // TS) == 0)
    def _():
        zbuf[0:8, :] = jnp.zeros((8, CONV_CH), F32)

    z = gc * xc
    zbuf[8:8 + TS, :] = z
    z1 = zbuf[7:7 + TS, :]
    z2 = zbuf[6:6 + TS, :]
    cw = cw_ref[...]
    conv = cw[0:1, :] * z2 + cw[1:2, :] * z1 + cw[2:3, :] * z
    ya_ref[...] = (gb * conv).astype(BF16)
    zbuf[0:8, :] = zbuf[TS:TS + 8, :]

    c = c_ref[...]
    s = s_ref[...]
    lane = lax.broadcasted_iota(I32, (TS, LANES), 1)
    first_half = lane < ROPE_LO + ROPE_HALF

    cqn = _rms(cq, qn_ref[...]).astype(BF16)
    q = jnp.dot(cqn, wq_ref[...], preferred_element_type=F32)
    for hd in range(MLA_HEADS):
        sl = slice(hd * HEAD_PAD, (hd + 1) * HEAD_PAD)
        q_ref[:, sl] = (_rope(q[:, sl], c, s, first_half) * Q_SCALE).astype(BF16)

    ckvn = _rms(ckv, kvn_ref[...]).astype(BF16)
    kv = jnp.dot(ckvn, wkv_ref[...], preferred_element_type=F32)
    kper = _rope(kpe, c, s, first_half)
    for hd in range(MLA_HEADS):
        sl = slice(hd * HEAD_PAD, (hd + 1) * HEAD_PAD)
        k_ref[:, sl] = (kv[:, sl] + kper).astype(BF16)
    v_ref[...] = kv[:, MLA_HEADS * HEAD_PAD:].astype(BF16)


def _even_front(h, g, win, cw, qn, wq, kvn, wkv, ctab, stab):
    tok = lambda w: pl.BlockSpec((TS, w), lambda i: (i, 0))
    return pl.pallas_call(
        _even_front_kernel,
        grid=(TOKENS // TS,),
        in_specs=[tok(D_MODEL), _full((1, D_MODEL)), _full(win.shape), _full(cw.shape),
                  _full((1, Q_RANK)), _full(wq.shape), _full((1, KV_RANK)), _full(wkv.shape),
                  tok(LANES), tok(LANES)],
        out_specs=[tok(CONV_CH), tok(MLA_HEADS * HEAD_PAD), tok(MLA_HEADS * HEAD_PAD), tok(MLA_HEADS * V_DIM)],
        out_shape=[jax.ShapeDtypeStruct((TOKENS, CONV_CH), BF16),
                   jax.ShapeDtypeStruct((TOKENS, MLA_HEADS * HEAD_PAD), BF16),
                   jax.ShapeDtypeStruct((TOKENS, MLA_HEADS * HEAD_PAD), BF16),
                   jax.ShapeDtypeStruct((TOKENS, MLA_HEADS * V_DIM), BF16)],
        scratch_shapes=[pltpu.VMEM((TS + 8, CONV_CH), F32)],
        compiler_params=_cparams(),
        name="even_front",
    )(h, g, win, cw, qn, wq, kvn, wkv, ctab, stab)


def _attn_kernel(q_ref, k_ref, v_ref, o_ref):
    i = pl.program_id(2)
    row = lax.broadcasted_iota(I32, (TQ, TQ), 0)
    col = lax.broadcasted_iota(I32, (TQ, TQ), 1)
    causal = col <= row
    nt = (((1,), (1,)), ((), ()))
    for hh in range(2):
        hs = slice(hh * HEAD_PAD, (hh + 1) * HEAD_PAD)
        q = q_ref[:, hs]

        def step(j, carry, masked):
            m, l, acc = carry
            kb = k_ref[pl.ds(pl.multiple_of(j * TQ, TQ), TQ), hs]
            vb = v_ref[pl.ds(pl.multiple_of(j * TQ, TQ), TQ), :]
            s = lax.dot_general(q, kb, nt, preferred_element_type=F32)
            if masked:
                s = jnp.where(causal, s, NEG_INF)
            m_new = jnp.maximum(m, jnp.max(s, axis=-1, keepdims=True))
            alpha = jnp.exp2(m - m_new)
            p = jnp.exp2(s - m_new)
            l = alpha * l + jnp.sum(p, axis=-1, keepdims=True)
            acc = alpha * acc + jnp.dot(p.astype(BF16), vb, preferred_element_type=F32)
            return m_new, l, acc

        init = (jnp.full((TQ, 1), NEG_INF, F32), jnp.zeros((TQ, 1), F32), jnp.zeros((TQ, 2 * V_DIM), F32))
        carry = lax.fori_loop(0, i, lambda j, cr: step(j, cr, False), init)
        m, l, acc = step(i, carry, True)
        vs = slice(hh * V_DIM, (hh + 1) * V_DIM)
        o_ref[:, vs] = (acc / l)[:, vs].astype(BF16)


def _attention(q, k, v):
    nq = SEQ // TQ
    pairs = MLA_HEADS // 2
    return pl.pallas_call(
        _attn_kernel,
        grid=(BATCH, pairs, nq),
        in_specs=[pl.BlockSpec((TQ, 2 * HEAD_PAD), lambda b, p, i: (b * nq + i, p)),
                  pl.BlockSpec((SEQ, 2 * HEAD_PAD), lambda b, p, i: (b, p)),
                  pl.BlockSpec((SEQ, 2 * V_DIM), lambda b, p, i: (b, p))],
        out_specs=pl.BlockSpec((TQ, 2 * V_DIM), lambda b, p, i: (b * nq + i, p)),
        out_shape=jax.ShapeDtypeStruct((TOKENS, MLA_HEADS * V_DIM), BF16),
        compiler_params=_cparams(3),
        name="mla_attention",
    )(q, k, v)


def _route_vectors(h1, gffn, wr_hi, wr_lo, carry):
    xn = _rms(h1, gffn)
    x_hi = xn.astype(BF16)
    x_lo = (xn - x_hi.astype(F32)).astype(BF16)
    logits = (jnp.dot(x_hi, wr_hi, preferred_element_type=F32)
              + jnp.dot(x_lo, wr_hi, preferred_element_type=F32)
              + jnp.dot(x_hi, wr_lo, preferred_element_type=F32))
    lane = lax.broadcasted_iota(I32, logits.shape, 1).astype(F32)
    big = float(LANES)

    is_g = lane < N_GROUPS
    gl = jnp.where(is_g, logits, NEG_INF)
    gmax = jnp.max(gl, axis=-1, keepdims=True)
    gidx = jnp.min(jnp.where(gl == gmax, lane, big), axis=-1, keepdims=True)
    gsum = jnp.sum(jnp.where(is_g, jnp.exp(gl - gmax), 0.0), axis=-1, keepdims=True)
    g_w = 1.0 / gsum

    lo = ROUTE_LO + gidx * EXP_PER_GROUP
    el = jnp.where((lane >= lo) & (lane < lo + EXP_PER_GROUP), logits, NEG_INF)
    m1 = jnp.max(el, axis=-1, keepdims=True)
    i1 = jnp.min(jnp.where(el == m1, lane, big), axis=-1, keepdims=True)
    el2 = jnp.where(lane == i1, NEG_INF, el)
    m2 = jnp.max(el2, axis=-1, keepdims=True)
    i2 = jnp.min(jnp.where(el2 == m2, lane, big), axis=-1, keepdims=True)
    t = jnp.exp(m2 - m1)
    w1 = g_w / (1.0 + t)
    w2 = g_w * t / (1.0 + t)

    oh1 = lane == i1
    oh2 = lane == i2
    oh = jnp.where(oh1 | oh2, 1.0, 0.0)
    r = lax.broadcasted_iota(I32, (TS, TS), 0)
    cc = lax.broadcasted_iota(I32, (TS, TS), 1)
    tri = jnp.where(cc < r, 1.0, 0.0).astype(BF16)
    before = jnp.dot(tri, oh.astype(BF16), preferred_element_type=F32) + carry[...]
    rank1 = jnp.sum(jnp.where(oh1, before, 0.0), axis=-1, keepdims=True)
    rank2 = jnp.sum(jnp.where(oh2, before, 0.0), axis=-1, keepdims=True)
    carry[...] = carry[...] + jnp.sum(oh, axis=0, keepdims=True)

    hi1 = jnp.floor(rank1 * (1.0 / BM))
    hi2 = jnp.floor(rank2 * (1.0 / BM))
    vals = (i1 - ROUTE_LO, i2 - ROUTE_LO, w1, w2,
            (i1 - ROUTE_LO) * NJ + hi1, (i2 - ROUTE_LO) * NJ + hi2, rank1 - hi1 * BM, rank2 - hi2 * BM)
    route = jnp.zeros_like(logits)
    for n, val in enumerate(vals):
        route = jnp.where(lane == float(n), val, route)
    route_t = route.T
    rows = [route_t[n:n + 1, part * LANES:(part + 1) * LANES] for n in (4, 5, 6, 7) for part in range(PARTS)]
    rows.append(carry[...])
    rows.append(jnp.zeros((STG_ROWS - CNT_ROW - 1, LANES), F32))
    return xn, route, jnp.concatenate(rows, axis=0).astype(I32)


def _moe_front(mixer, gffn_ref, wrh_ref, wrl_ref, route_ref, xs_hbm, dest_out, tab_out, cnt_out, nal_out,
               carry, xns, stg_v, stg_s, tab, cntp, nal, ssem, gsem, zsem):
    i = pl.program_id(0)
    n = pl.num_programs(0)
    last_slot = X_SLOTS - 1

    def stage_copy(slot):
        return pltpu.make_async_copy(stg_v, stg_s.at[slot], ssem)

    def hand_out_chunks(slot):
        def body(e, na):
            c_new = stg_s[slot, CNT_ROW, ROUTE_LO + e]
            cb = (cntp[e] + (BM - 1)) >> BM_SHIFT
            ca = (c_new + (BM - 1)) >> BM_SHIFT
            tab[e * NJ + cb] = na
            cntp[e] = c_new
            return na + (ca - cb)

        nal[0] = lax.fori_loop(0, N_EXPERTS, body, nal[0])

    def dest_row(slot, which, part, lane):
        chunk = tab[stg_s[slot, which * PARTS + part, lane]]
        return (chunk << BM_SHIFT) + stg_s[slot, (2 + which) * PARTS + part, lane]

    def scatter_row(slot, tile, which, part, lane):
        r = part * LANES + lane
        d = dest_row(slot, which, part, lane)
        dest_out[which * DEST_HALF + (tile + 1) * TS + r] = d
        _row_copy(xns[slot], r, xs_hbm, d, gsem.at[slot]).start()

    def wait_scatter(slot):
        def body(r, c):
            _row_copy(xns[slot], 0, xs_hbm, 0, gsem.at[slot]).wait()
            return c

        lax.fori_loop(0, 2 * TS, body, 0, unroll=8)

    @pl.when(i == 0)
    def _():
        carry[...] = jnp.zeros_like(carry)
        xns[last_slot][...] = jnp.zeros((TS, D_MODEL), F32)

        for which in range(2):
            for part in range(PARTS):
                def placeholder(l, c, which=which, part=part):
                    stg_s[last_slot, which * PARTS + part, l] = N_EXPERTS * NJ + which
                    stg_s[last_slot, (2 + which) * PARTS + part, l] = part * LANES + l
                    return c

                lax.fori_loop(0, LANES, placeholder, 0)

        def clear_tab(j, c):
            tab[j] = 0
            return c

        lax.fori_loop(0, N_EXPERTS * NJ, clear_tab, 0)
        tab[N_EXPERTS * NJ] = N_CHUNKS
        tab[N_EXPERTS * NJ + 1] = N_CHUNKS + 1

        def clear_cnt(e, c):
            cntp[e] = 0
            return c

        lax.fori_loop(0, N_EXPERTS, clear_cnt, 0)
        nal[0] = 0

    def variant(k):
        prev = (k + X_SLOTS - 1) % X_SLOTS

        def run():
            @pl.when(i >= 1)
            def _():
                stage_copy(prev).wait()
                hand_out_chunks(prev)

            @pl.when(i >= X_SLOTS)
            def _():
                wait_scatter(k)

            for part in range(PARTS):
                for lane in range(LANES):
                    for which in range(2):
                        scatter_row(prev, i - 1, which, part, lane)
            xn, route, stage = _route_vectors(mixer(), gffn_ref[...], wrh_ref[...], wrl_ref[...], carry)
            xns[k][...] = xn
            route_ref[...] = route
            stg_v[...] = stage
            stage_copy(k).start()

        return run

    lax.switch(i % X_SLOTS, [variant(k) for k in range(X_SLOTS)])

    @pl.when(i == 0)
    def _():
        wait_scatter(last_slot)

    @pl.when(i == n - 1)
    def _():
        for k in range(X_SLOTS):
            @pl.when(i % X_SLOTS == k)
            def _():
                stage_copy(k).wait()
                hand_out_chunks(k)
                for part in range(PARTS):
                    def scatter_lane(lane, c, part=part):
                        for which in range(2):
                            scatter_row(k, i, which, part, lane)
                        return c

                    lax.fori_loop(0, LANES, scatter_lane, 0)

        for k in range(X_SLOTS):
            wait_scatter(k)

        xns[0][...] = jnp.zeros((TS, D_MODEL), F32)

        def pad_expert(e, c):
            used = cntp[e] & (BM - 1)
            base = tab[e * NJ + (cntp[e] >> BM_SHIFT)] << BM_SHIFT
            stop = jnp.where(used > 0, BM, 0)
            lax.fori_loop(used, stop, lambda r, c2: (_row_copy(xns[0], 0, xs_hbm, base + r, zsem).start(), c2)[1], 0)
            lax.fori_loop(used, stop, lambda r, c2: (_row_copy(xns[0], 0, xs_hbm, base + r, zsem).wait(), c2)[1], 0)
            cnt_out[e] = cntp[e]
            return c

        lax.fori_loop(0, N_EXPERTS, pad_expert, 0)

        def spare(chunk):
            return pltpu.make_async_copy(xns[0], xs_hbm.at[_chunk_rows(chunk)], zsem)

        lax.fori_loop(nal[0], N_CHUNKS, lambda ch, c: (spare(ch).start(), c)[1], 0)
        lax.fori_loop(nal[0], N_CHUNKS, lambda ch, c: (spare(ch).wait(), c)[1], 0)

        def copy_tab(j, c):
            tab_out[j] = tab[j]
            return c

        lax.fori_loop(0, N_EXPERTS * NJ, copy_tab, 0)
        nal_out[0] = nal[0]


_FRONT_OUT_SHAPES = [jax.ShapeDtypeStruct((TOKENS, D_MODEL), F32),
                     jax.ShapeDtypeStruct((TOKENS, LANES), F32),
                     jax.ShapeDtypeStruct((XS_ROWS, D_MODEL), F32),
                     jax.ShapeDtypeStruct((2 * DEST_HALF,), I32),
                     jax.ShapeDtypeStruct((N_EXPERTS * NJ,), I32),
                     jax.ShapeDtypeStruct((N_EXPERTS,), I32),
                     jax.ShapeDtypeStruct((1,), I32)]


def _front_out_specs():
    tok = lambda w: pl.BlockSpec((TS, w), lambda i: (i, 0))
    smem = lambda: pl.BlockSpec(memory_space=pltpu.SMEM)
    return [tok(D_MODEL), tok(LANES), pl.BlockSpec(memory_space=pl.ANY), smem(), smem(), smem(), smem()]


def _front_scratch():
    return ([pltpu.VMEM((1, LANES), F32)] + [pltpu.VMEM((TS, D_MODEL), F32)] * X_SLOTS
            + [pltpu.VMEM((STG_ROWS, LANES), I32), pltpu.SMEM((X_SLOTS, STG_ROWS, LANES), I32),
               pltpu.SMEM((N_EXPERTS * NJ + 2,), I32), pltpu.SMEM((N_EXPERTS,), I32), pltpu.SMEM((1,), I32),
               pltpu.SemaphoreType.DMA, pltpu.SemaphoreType.DMA((X_SLOTS,)), pltpu.SemaphoreType.DMA])


def _split_front_scratch(scratch):
    carry = scratch[0]
    xns = tuple(scratch[1:1 + X_SLOTS])
    return (carry, xns) + tuple(scratch[1 + X_SLOTS:])


def _even_out_kernel(ya_ref, yb_ref, wo_ref, h_ref, gffn_ref, wrh_ref, wrl_ref,
                     h1_ref, route_ref, xs_hbm, dest_out, tab_out, cnt_out, nal_out, *scratch):
    def mixer():
        mix = (jnp.dot(ya_ref[...], wo_ref[0:CONV_CH, :], preferred_element_type=F32)
               + jnp.dot(yb_ref[...], wo_ref[CONV_CH:, :], preferred_element_type=F32))
        h1 = h_ref[...] + mix
        h1_ref[...] = h1
        return h1

    _moe_front(mixer, gffn_ref, wrh_ref, wrl_ref, route_ref, xs_hbm, dest_out, tab_out, cnt_out, nal_out,
               *_split_front_scratch(scratch))


def _even_out(ya, yb, wo, h, gffn, wrh, wrl):
    tok = lambda w: pl.BlockSpec((TS, w), lambda i: (i, 0))
    return pl.pallas_call(
        _even_out_kernel,
        grid=(N_TILES,),
        in_specs=[tok(CONV_CH), tok(MLA_HEADS * V_DIM), _full(wo.shape), tok(D_MODEL),
                  _full((1, D_MODEL)), _full(wrh.shape), _full(wrl.shape)],
        out_specs=_front_out_specs(),
        out_shape=_FRONT_OUT_SHAPES,
        scratch_shapes=_front_scratch(),
        compiler_params=_cparams(),
        name="even_out_route",
    )(ya, yb, wo, h, gffn, wrh, wrl)


def _gelu_tanh(x):
    return 0.5 * x * (1.0 + jnp.tanh(math.sqrt(2.0 / math.pi) * (x + 0.044715 * (x * x * x))))


def _odd_kernel(h_ref, g_ref, win_ref, vn_ref, ws_ref, bs_ref, wo_ref, gffn_ref, wrh_ref, wrl_ref,
                h1_ref, route_ref, xs_hbm, dest_out, tab_out, cnt_out, nal_out, mixbuf, *scratch):
    def mixer():
        h = h_ref[...]
        hn = _rms(h, g_ref[...]).astype(BF16)
        z = _gelu_tanh(jnp.dot(hn, win_ref[...], preferred_element_type=F32))
        u = z[:, :SG_WIDTH]
        v = _rms(z[:, SG_WIDTH:], vn_ref[...]).astype(BF16)
        r = lax.broadcasted_iota(I32, (SG_CHUNK, SG_CHUNK), 0)
        c = lax.broadcasted_iota(I32, (SG_CHUNK, SG_CHUNK), 1)
        n_chunks = TS // SG_CHUNK
        dg = SG_WIDTH // SG_GROUPS
        for gi in range(SG_GROUPS):
            gs = slice(gi * dg, (gi + 1) * dg)
            wtril = jnp.where(c <= r, ws_ref[gi], 0.0).astype(BF16)
            rhs = jnp.concatenate([v[ci * SG_CHUNK:(ci + 1) * SG_CHUNK, gs] for ci in range(n_chunks)], axis=1)
            sg = jnp.dot(wtril, rhs, preferred_element_type=F32)
            for ci in range(n_chunks):
                rs = slice(ci * SG_CHUNK, (ci + 1) * SG_CHUNK)
                gate = sg[:, ci * dg:(ci + 1) * dg] + bs_ref[:, gs]
                mixbuf[rs, gs] = (u[rs, gs] * gate).astype(BF16)
        h1 = h + jnp.dot(mixbuf[...], wo_ref[...], preferred_element_type=F32)
        h1_ref[...] = h1
        return h1

    _moe_front(mixer, gffn_ref, wrh_ref, wrl_ref, route_ref, xs_hbm, dest_out, tab_out, cnt_out, nal_out,
               *_split_front_scratch(scratch))


def _odd_layer(h, g, win, vn, ws, bs_full, wo, gffn, wrh, wrl):
    tok = lambda w: pl.BlockSpec((TS, w), lambda i: (i, 0))
    return pl.pallas_call(
        _odd_kernel,
        grid=(N_TILES,),
        in_specs=[tok(D_MODEL), _full((1, D_MODEL)), _full(win.shape), _full((1, SG_WIDTH)),
                  _full(ws.shape), _full(bs_full.shape), _full(wo.shape),
                  _full((1, D_MODEL)), _full(wrh.shape), _full(wrl.shape)],
        out_specs=_front_out_specs(),
        out_shape=_FRONT_OUT_SHAPES,
        scratch_shapes=[pltpu.VMEM((TS, SG_WIDTH), BF16)] + _front_scratch(),
        compiler_params=_cparams(),
        name="odd_mixer_route",
    )(h, g, win, vn, ws, bs_full, wo, gffn, wrh, wrl)


def _gmm_kernel(tab_ref, cnt_ref, nal_ref, wg_ref, wu_ref, wd_ref, xs_hbm, yb_hbm,
                wgb, wub, wdb, xbuf, ybuf, order, first, xsem, ysem, zsem):
    e = pl.program_id(0)
    nblk = nal_ref[0]

    @pl.when(e == 0)
    def _():
        def list_expert(ex, g):
            first[ex] = g
            nb = (cnt_ref[ex] + (BM - 1)) >> BM_SHIFT

            def put(j, c):
                order[g + j] = tab_ref[ex * NJ + j]
                return c

            lax.fori_loop(0, nb, put, 0)
            return g + nb

        first[N_EXPERTS] = lax.fori_loop(0, N_EXPERTS, list_expert, 0)

    def x_copy(g, slot):
        return pltpu.make_async_copy(xs_hbm.at[_chunk_rows(order[g])], xbuf.at[slot], xsem.at[slot])

    def y_copy(g, slot):
        return pltpu.make_async_copy(ybuf.at[slot], yb_hbm.at[_chunk_rows(order[g])], ysem.at[slot])

    @pl.when(e == 0)
    def _():
        for a in range(AHEAD):
            @pl.when(a < nblk)
            def _():
                x_copy(a, a).start()

    g0 = first[e]
    g1 = first[e + 1]

    @pl.when(g1 > g0)
    def _():
        wgb[...] = wg_ref[0].astype(BF16)
        wub[...] = wu_ref[0].astype(BF16)
        wdb[...] = wd_ref[0].astype(BF16)

        def block(g, c):
            slot = g % X_SLOTS
            ys = g & 1

            @pl.when(g + AHEAD < nblk)
            def _():
                x_copy(g + AHEAD, (g + AHEAD) % X_SLOTS).start()

            x_copy(g, slot).wait()

            @pl.when(g >= 2)
            def _():
                y_copy(g - 2, ys).wait()

            x = xbuf[slot].astype(BF16)
            gate = jnp.dot(x, wgb[...], preferred_element_type=F32)
            up = jnp.dot(x, wub[...], preferred_element_type=F32)
            mid = (gate * jax.nn.sigmoid(gate) * up).astype(BF16)
            ybuf[ys] = jnp.dot(mid, wdb[...], preferred_element_type=F32)
            y_copy(g, ys).start()
            return c

        lax.fori_loop(g0, g1, block, 0)

    @pl.when(e == pl.num_programs(0) - 1)
    def _():
        @pl.when(nblk >= 2)
        def _():
            y_copy(nblk - 2, nblk & 1).wait()

        y_copy(nblk - 1, (nblk - 1) & 1).wait()
        xbuf[0] = jnp.zeros((BM, D_MODEL), F32)

        def spare(chunk):
            return pltpu.make_async_copy(xbuf.at[0], yb_hbm.at[_chunk_rows(chunk)], zsem)

        lax.fori_loop(nblk, N_CHUNKS, lambda ch, c: (spare(ch).start(), c)[1], 0)
        lax.fori_loop(nblk, N_CHUNKS, lambda ch, c: (spare(ch).wait(), c)[1], 0)


def _gmm(tab, cnt, nal, xs, w_gate, w_up, w_down, layer):
    wsel = lambda e, *_: (layer * N_EXPERTS + e, 0, 0)
    hbm = pl.BlockSpec(memory_space=pl.ANY)
    return pl.pallas_call(
        _gmm_kernel,
        grid_spec=pltpu.PrefetchScalarGridSpec(
            num_scalar_prefetch=3,
            grid=(N_EXPERTS,),
            in_specs=[pl.BlockSpec((1, D_MODEL, D_EXPERT), wsel),
                      pl.BlockSpec((1, D_MODEL, D_EXPERT), wsel),
                      pl.BlockSpec((1, D_EXPERT, D_MODEL), wsel),
                      hbm],
            out_specs=hbm,
            scratch_shapes=[pltpu.VMEM((D_MODEL, D_EXPERT), BF16), pltpu.VMEM((D_MODEL, D_EXPERT), BF16),
                            pltpu.VMEM((D_EXPERT, D_MODEL), BF16),
                            pltpu.VMEM((X_SLOTS, BM, D_MODEL), F32), pltpu.VMEM((2, BM, D_MODEL), F32),
                            pltpu.SMEM((N_CHUNKS,), I32), pltpu.SMEM((N_EXPERTS + 1,), I32),
                            pltpu.SemaphoreType.DMA((X_SLOTS,)), pltpu.SemaphoreType.DMA((2,)),
                            pltpu.SemaphoreType.DMA],
        ),
        out_shape=jax.ShapeDtypeStruct((N_CHUNKS * BM, D_MODEL), F32),
        compiler_params=_cparams(),
        name="moe_grouped_matmul",
    )(tab, cnt, nal, w_gate, w_up, w_down, xs)


def _combine_ple_kernel(dest_ref, h1_ref, route_ref, p_ref, gple_ref, wgate_ref, wproj_ref, fin_ref,
                        yb_hbm, out_ref, buf0, buf1, buf2, buf3, sem, *, final):
    i = pl.program_id(0)
    n = pl.num_programs(0)
    bufs = (buf0, buf1, buf2, buf3)

    def copies(tile, r, slot):
        t = (tile + 1) * TS + r
        return [_row_copy(yb_hbm, dest_ref[which * DEST_HALF + t], bufs[slot], which * TS + r, sem.at[slot])
                for which in range(2)]

    def issue_loop(tile, slot):
        def body(r, c):
            for cp in copies(tile, r, slot):
                cp.start()
            return c

        lax.fori_loop(0, TS, body, 0)

    def issue_unrolled(tile, slot):
        for r in range(TS):
            for cp in copies(tile, r, slot):
                cp.start()

    def wait_tile(slot):
        def body(r, c):
            _row_copy(yb_hbm, 0, bufs[slot], 0, sem.at[slot]).wait()
            return c

        lax.fori_loop(0, 2 * TS, body, 0, unroll=8)

    def compute(quarter, slot):
        rows = slice(quarter * TS, (quarter + 1) * TS)
        route = route_ref[rows, :]
        w1 = route[:, 2:3]
        w2 = route[:, 3:4]
        h2 = h1_ref[rows, :] + (w1 * bufs[slot][0:TS, :] + w2 * bufs[slot][TS:2 * TS, :])
        gate = jax.nn.sigmoid(jnp.dot(_rms(h2, gple_ref[...]).astype(BF16), wgate_ref[...],
                                      preferred_element_type=F32))
        out = h2 + gate * jnp.dot(p_ref[rows, :].astype(BF16), wproj_ref[...], preferred_element_type=F32)
        if final:
            out = _rms(out, fin_ref[...])
        out_ref[rows, :] = out

    @pl.when(i == 0)
    def _():
        for k in range(AHEAD):
            issue_loop(k, k)

    for k in range(X_SLOTS):
        tile = X_SLOTS * i + k
        ahead = jnp.minimum(tile + AHEAD, N_TILES - 1)
        wait_tile(k)
        issue_unrolled(ahead, (k + AHEAD) % X_SLOTS)
        compute(k, k)

    @pl.when(i == n - 1)
    def _():
        for k in range(AHEAD):
            wait_tile((N_TILES + k) % X_SLOTS)


def _combine_ple(dest, h1, route, p_all, layer, gple, wgate, wproj, fin, yb, final):
    nt = TOKENS // (X_SLOTS * TS)
    tok = lambda w: pl.BlockSpec((X_SLOTS * TS, w), lambda i, *_: (i, 0))
    full = lambda shape: pl.BlockSpec(shape, lambda i, *_: (0,) * len(shape))
    return pl.pallas_call(
        functools.partial(_combine_ple_kernel, final=final),
        grid_spec=pltpu.PrefetchScalarGridSpec(
            num_scalar_prefetch=1,
            grid=(nt,),
            in_specs=[tok(D_MODEL), tok(LANES),
                      pl.BlockSpec((X_SLOTS * TS, PLE_DIM), lambda i, *_: (layer * nt + i, 0)),
                      full((1, D_MODEL)), full(wgate.shape), full(wproj.shape), full((1, D_MODEL)),
                      pl.BlockSpec(memory_space=pl.ANY)],
            out_specs=tok(D_MODEL),
            scratch_shapes=[pltpu.VMEM((2 * TS, D_MODEL), F32)] * X_SLOTS + [pltpu.SemaphoreType.DMA((X_SLOTS,))],
        ),
        out_shape=jax.ShapeDtypeStruct((TOKENS, D_MODEL), F32),
        compiler_params=_cparams(),
        name="moe_combine_ple",
    )(dest, h1, route, p_all, gple, wgate, wproj, fin, yb)


def _router_weights(w_rg, w_re):
    w = jnp.zeros((D_MODEL, LANES), F32).at[:, :N_GROUPS].set(w_rg).at[:, ROUTE_LO:ROUTE_LO + N_EXPERTS].set(w_re)
    hi = w.astype(BF16)
    lo = (w - hi.astype(F32)).astype(BF16)
    return hi, lo


def _even_weights(w_in, w_q_up, w_kv_up):
    base = 3 * CONV_CH + Q_RANK + KV_RANK
    kpe_cols = jnp.zeros((D_MODEL, HEAD_PAD), F32).at[:, ROPE_LO:ROPE_LO + QK_ROPE].set(w_in[:, base:])
    win = jnp.concatenate([w_in[:, :base], kpe_cols], axis=1).astype(BF16)
    dqk = QK_NOPE + QK_ROPE
    wq = jnp.pad(w_q_up.reshape(Q_RANK, MLA_HEADS, dqk), ((0, 0), (0, 0), (0, HEAD_PAD - dqk)))
    wq = wq.reshape(Q_RANK, MLA_HEADS * HEAD_PAD).astype(BF16)
    wkv = w_kv_up.reshape(KV_RANK, MLA_HEADS, QK_NOPE + V_DIM)
    wk = jnp.pad(wkv[:, :, :QK_NOPE], ((0, 0), (0, 0), (0, HEAD_PAD - QK_NOPE))).reshape(KV_RANK, MLA_HEADS * HEAD_PAD)
    wv = wkv[:, :, QK_NOPE:].reshape(KV_RANK, MLA_HEADS * V_DIM)
    return win, wq, jnp.concatenate([wk, wv], axis=1).astype(BF16)


def kernel(x, p, positions, norm_mix, norm_ffn, w_in_e, conv_w, q_norm, w_q_up, kv_norm, w_kv_up, w_out_e,
           w_in_o, v_norm, w_s, b_s, w_out_o, w_router_group, w_router_expert, w_gate, w_up, w_down,
           norm_ple, w_ple_gate, w_ple_proj, final_norm):
    h = x.reshape(TOKENS, D_MODEL)
    p_all = p.reshape(DEPTH * TOKENS, PLE_DIM)
    wg_all = w_gate.reshape(DEPTH * N_EXPERTS, D_MODEL, D_EXPERT)
    wu_all = w_up.reshape(DEPTH * N_EXPERTS, D_MODEL, D_EXPERT)
    wd_all = w_down.reshape(DEPTH * N_EXPERTS, D_EXPERT, D_MODEL)
    ctab, stab = _rope_tables(positions)
    row = lambda a: a.reshape(1, -1)

    for i in range(DEPTH):
        j = i // 2
        wrh, wrl = _router_weights(w_router_group[i], w_router_expert[i])
        if i % 2 == 0:
            win, wq, wkv = _even_weights(w_in_e[j], w_q_up[j], w_kv_up[j])
            ya, q, k, v = _even_front(h, row(norm_mix[i]), win, conv_w[j], row(q_norm[j]), wq,
                                      row(kv_norm[j]), wkv, ctab, stab)
            yb = _attention(q, k, v)
            front = _even_out(ya, yb, w_out_e[j].astype(BF16), h, row(norm_ffn[i]), wrh, wrl)
        else:
            bs_full = jnp.repeat(b_s[j].T, SG_WIDTH // SG_GROUPS, axis=1)
            front = _odd_layer(h, row(norm_mix[i]), w_in_o[j].astype(BF16), row(v_norm[j]), w_s[j], bs_full,
                               w_out_o[j].astype(BF16), row(norm_ffn[i]), wrh, wrl)
        h1, route, xs, dest, tab, cnt, nal = front
        ys = _gmm(tab, cnt, nal, xs, wg_all, wu_all, wd_all, i)
        h = _combine_ple(dest, h1, route, p_all, i, row(norm_ple[i]), w_ple_gate[i].astype(BF16),
                         w_ple_proj[i].astype(BF16), row(final_norm), ys, final=(i == DEPTH - 1))
    return h.reshape(BATCH, SEQ, D_MODEL)
```

```python
import functools
import math

import jax
import jax.numpy as jnp
from jax import lax
from jax.experimental import pallas as pl
from jax.experimental.pallas import tpu as pltpu

F32 = jnp.float32
BF16 = jnp.bfloat16
I32 = jnp.int32

D_MODEL = 1024
BATCH = 4
SEQ = 4096
DEPTH = 4
TOKENS = BATCH * SEQ

CONV_CH = 512
MLA_HEADS = 8
QK_NOPE = 64
QK_ROPE = 32
V_DIM = 64
Q_RANK = 384
KV_RANK = 256
ROPE_THETA = 10000.0
SG_WIDTH = 1024
SG_GROUPS = 8
SG_CHUNK = 128
N_GROUPS = 4
EXP_PER_GROUP = 8
N_EXPERTS = N_GROUPS * EXP_PER_GROUP
D_EXPERT = 512
PLE_DIM = 256
NORM_EPS = 1e-6
NEG_INF = -1e30

LANES = 128
HEAD_PAD = LANES
ROPE_LO = QK_NOPE
ROPE_HALF = QK_ROPE // 2

TS = 256
PARTS = TS // LANES
TQ = 1024
BM = 256
BM_SHIFT = BM.bit_length() - 1
assert 1 << BM_SHIFT == BM and TS == BM
N_TILES = TOKENS // TS
N_CHUNKS = 2 * TOKENS // BM + N_EXPERTS
NJ = TOKENS // BM + 1
XS_ROWS = (N_CHUNKS + 2) * BM
SCATTER_LAG = 2
DEST_HALF = TOKENS + SCATTER_LAG * TS
ROUTE_LO = N_GROUPS
STG_ROWS = 16
CNT_ROW = 4 * PARTS
X_SLOTS = 4
AHEAD = 2

VMEM_LIMIT = 48 * 1024 * 1024

Q_SCALE = math.log2(math.e) / math.sqrt(QK_NOPE + QK_ROPE)


def _cparams(n_axes=1):
    return pltpu.CompilerParams(dimension_semantics=("arbitrary",) * n_axes,
                                vmem_limit_bytes=VMEM_LIMIT)


def _rms(x, g):
    return x * lax.rsqrt(jnp.mean(x * x, axis=-1, keepdims=True) + NORM_EPS) * g


def _full(shape):
    return pl.BlockSpec(shape, lambda *_: (0,) * len(shape))


def _row_copy(src, si, dst, di, sem):
    return pltpu.make_async_copy(src.at[pl.ds(si, 1)], dst.at[pl.ds(di, 1)], sem)


def _chunk_rows(chunk):
    return pl.ds(pl.multiple_of(chunk * BM, BM), BM)


def _rope_table_kernel(pos_ref, freq_ref, c_ref, s_ref):
    ang = pos_ref[...].astype(F32) * freq_ref[...]
    lane = lax.broadcasted_iota(I32, ang.shape, 1)
    cosv = jnp.cos(ang)
    sinv = jnp.sin(ang)
    in_rope = (lane >= ROPE_LO) & (lane < ROPE_LO + QK_ROPE)
    first_half = lane < ROPE_LO + ROPE_HALF
    c_ref[...] = jnp.where(lane < ROPE_LO, 1.0, jnp.where(in_rope, cosv, 0.0))
    s_ref[...] = jnp.where(in_rope, jnp.where(first_half, -sinv, sinv), 0.0)


def _rope_tables(positions):
    inv_freq = 1.0 / (ROPE_THETA ** (jnp.arange(0, QK_ROPE, 2, dtype=F32) / QK_ROPE))
    freq = jnp.zeros((LANES,), F32).at[ROPE_LO:ROPE_LO + QK_ROPE].set(jnp.tile(inv_freq, 2))
    pos = positions.reshape(TOKENS, 1)
    return pl.pallas_call(
        _rope_table_kernel,
        grid=(TOKENS // TS,),
        in_specs=[pl.BlockSpec((TS, 1), lambda i: (i, 0)), _full((1, LANES))],
        out_specs=[pl.BlockSpec((TS, LANES), lambda i: (i, 0))] * 2,
        out_shape=[jax.ShapeDtypeStruct((TOKENS, LANES), F32)] * 2,
        compiler_params=_cparams(),
        name="rope_tables",
    )(pos, freq.reshape(1, LANES))


def _rope(x, c, s, first_half):
    partner = jnp.where(first_half, pltpu.roll(x, LANES - ROPE_HALF, 1), pltpu.roll(x, ROPE_HALF, 1))
    return x * c + partner * s


def _even_front_kernel(h_ref, g_ref, win_ref, cw_ref, qn_ref, wq_ref, kvn_ref, wkv_ref, c_ref, s_ref,
                       ya_ref, q_ref, k_ref, v_ref, zbuf):
    i = pl.program_id(0)
    hn = _rms(h_ref[...], g_ref[...]).astype(BF16)
    proj = jnp.dot(hn, win_ref[...], preferred_element_type=F32)
    xc = proj[:, 0:CONV_CH]
    gb = proj[:, CONV_CH:2 * CONV_CH]
    gc = proj[:, 2 * CONV_CH:3 * CONV_CH]
    o = 3 * CONV_CH
    cq = proj[:, o:o + Q_RANK]
    ckv = proj[:, o + Q_RANK:o + Q_RANK + KV_RANK]
    kpe = proj[:, o + Q_RANK + KV_RANK:]

    @pl.when(i % (SEQ // TS) == 0)
    def _():
        zbuf[0:8, :] = jnp.zeros((8, CONV_CH), F32)

    z = gc * xc
    zbuf[8:8 + TS, :] = z
    z1 = zbuf[7:7 + TS, :]
    z2 = zbuf[6:6 + TS, :]
    cw = cw_ref[...]
    conv = cw[0:1, :] * z2 + cw[1:2, :] * z1 + cw[2:3, :] * z
    ya_ref[...] = (gb * conv).astype(BF16)
    zbuf[0:8, :] = zbuf[TS:TS + 8, :]

    c = c_ref[...]
    s = s_ref[...]
    lane = lax.broadcasted_iota(I32, (TS, LANES), 1)
    first_half = lane < ROPE_LO + ROPE_HALF

    cqn = _rms(cq, qn_ref[...]).astype(BF16)
    q = jnp.dot(cqn, wq_ref[...], preferred_element_type=F32)
    for hd in range(MLA_HEADS):
        sl = slice(hd * HEAD_PAD, (hd + 1) * HEAD_PAD)
        q_ref[:, sl] = (_rope(q[:, sl], c, s, first_half) * Q_SCALE).astype(BF16)

    ckvn = _rms(ckv, kvn_ref[...]).astype(BF16)
    kv = jnp.dot(ckvn, wkv_ref[...], preferred_element_type=F32)
    kper = _rope(kpe, c, s, first_half)
    for hd in range(MLA_HEADS):
        sl = slice(hd * HEAD_PAD, (hd + 1) * HEAD_PAD)
        k_ref[:, sl] = (kv[:, sl] + kper).astype(BF16)
    v_ref[...] = kv[:, MLA_HEADS * HEAD_PAD:].astype(BF16)


def _even_front(h, g, win, cw, qn, wq, kvn, wkv, ctab, stab):
    tok = lambda w: pl.BlockSpec((TS, w), lambda i: (i, 0))
    return pl.pallas_call(
        _even_front_kernel,
        grid=(TOKENS // TS,),
        in_specs=[tok(D_MODEL), _full((1, D_MODEL)), _full(win.shape), _full(cw.shape),
                  _full((1, Q_RANK)), _full(wq.shape), _full((1, KV_RANK)), _full(wkv.shape),
                  tok(LANES), tok(LANES)],
        out_specs=[tok(CONV_CH), tok(MLA_HEADS * HEAD_PAD), tok(MLA_HEADS * HEAD_PAD), tok(MLA_HEADS * V_DIM)],
        out_shape=[jax.ShapeDtypeStruct((TOKENS, CONV_CH), BF16),
                   jax.ShapeDtypeStruct((TOKENS, MLA_HEADS * HEAD_PAD), BF16),
                   jax.ShapeDtypeStruct((TOKENS, MLA_HEADS * HEAD_PAD), BF16),
                   jax.ShapeDtypeStruct((TOKENS, MLA_HEADS * V_DIM), BF16)],
        scratch_shapes=[pltpu.VMEM((TS + 8, CONV_CH), F32)],
        compiler_params=_cparams(),
        name="even_front",
    )(h, g, win, cw, qn, wq, kvn, wkv, ctab, stab)


def _attn_kernel(q_ref, k_ref, v_ref, o_ref):
    i = pl.program_id(2)
    row = lax.broadcasted_iota(I32, (TQ, TQ), 0)
    col = lax.broadcasted_iota(I32, (TQ, TQ), 1)
    causal = col <= row
    nt = (((1,), (1,)), ((), ()))
    for hh in range(2):
        hs = slice(hh * HEAD_PAD, (hh + 1) * HEAD_PAD)
        q = q_ref[:, hs]

        def step(j, carry, masked):
            m, l, acc = carry
            kb = k_ref[pl.ds(pl.multiple_of(j * TQ, TQ), TQ), hs]
            vb = v_ref[pl.ds(pl.multiple_of(j * TQ, TQ), TQ), :]
            s = lax.dot_general(q, kb, nt, preferred_element_type=F32)
            if masked:
                s = jnp.where(causal, s, NEG_INF)
            m_new = jnp.maximum(m, jnp.max(s, axis=-1, keepdims=True))
            alpha = jnp.exp2(m - m_new)
            p = jnp.exp2(s - m_new)
            l = alpha * l + jnp.sum(p, axis=-1, keepdims=True)
            acc = alpha * acc + jnp.dot(p.astype(BF16), vb, preferred_element_type=F32)
            return m_new, l, acc

        init = (jnp.full((TQ, 1), NEG_INF, F32), jnp.zeros((TQ, 1), F32), jnp.zeros((TQ, 2 * V_DIM), F32))
        carry = lax.fori_loop(0, i, lambda j, cr: step(j, cr, False), init)
        m, l, acc = step(i, carry, True)
        vs = slice(hh * V_DIM, (hh + 1) * V_DIM)
        o_ref[:, vs] = (acc / l)[:, vs].astype(BF16)


def _attention(q, k, v):
    nq = SEQ // TQ
    pairs = MLA_HEADS // 2
    return pl.pallas_call(
        _attn_kernel,
        grid=(BATCH, pairs, nq),
        in_specs=[pl.BlockSpec((TQ, 2 * HEAD_PAD), lambda b, p, i: (b * nq + i, p)),
                  pl.BlockSpec((SEQ, 2 * HEAD_PAD), lambda b, p, i: (b, p)),
                  pl.BlockSpec((SEQ, 2 * V_DIM), lambda b, p, i: (b, p))],
        out_specs=pl.BlockSpec((TQ, 2 * V_DIM), lambda b, p, i: (b * nq + i, p)),
        out_shape=jax.ShapeDtypeStruct((TOKENS, MLA_HEADS * V_DIM), BF16),
        compiler_params=_cparams(3),
        name="mla_attention",
    )(q, k, v)


def _route_vectors(h1, gffn, wr_hi, wr_lo, carry):
    xn = _rms(h1, gffn)
    x_hi = xn.astype(BF16)
    x_lo = (xn - x_hi.astype(F32)).astype(BF16)
    logits = (jnp.dot(x_hi, wr_hi, preferred_element_type=F32)
              + jnp.dot(x_lo, wr_hi, preferred_element_type=F32)
              + jnp.dot(x_hi, wr_lo, preferred_element_type=F32))
    lane = lax.broadcasted_iota(I32, logits.shape, 1).astype(F32)
    big = float(LANES)

    is_g = lane < N_GROUPS
    gl = jnp.where(is_g, logits, NEG_INF)
    gmax = jnp.max(gl, axis=-1, keepdims=True)
    gidx = jnp.min(jnp.where(gl == gmax, lane, big), axis=-1, keepdims=True)
    gsum = jnp.sum(jnp.where(is_g, jnp.exp(gl - gmax), 0.0), axis=-1, keepdims=True)
    g_w = 1.0 / gsum

    lo = ROUTE_LO + gidx * EXP_PER_GROUP
    el = jnp.where((lane >= lo) & (lane < lo + EXP_PER_GROUP), logits, NEG_INF)
    m1 = jnp.max(el, axis=-1, keepdims=True)
    i1 = jnp.min(jnp.where(el == m1, lane, big), axis=-1, keepdims=True)
    el2 = jnp.where(lane == i1, NEG_INF, el)
    m2 = jnp.max(el2, axis=-1, keepdims=True)
    i2 = jnp.min(jnp.where(el2 == m2, lane, big), axis=-1, keepdims=True)
    t = jnp.exp(m2 - m1)
    w1 = g_w / (1.0 + t)
    w2 = g_w * t / (1.0 + t)

    oh1 = lane == i1
    oh2 = lane == i2
    oh = jnp.where(oh1 | oh2, 1.0, 0.0)
    r = lax.broadcasted_iota(I32, (TS, TS), 0)
    cc = lax.broadcasted_iota(I32, (TS, TS), 1)
    tri = jnp.where(cc < r, 1.0, 0.0).astype(BF16)
    before = jnp.dot(tri, oh.astype(BF16), preferred_element_type=F32) + carry[...]
    rank1 = jnp.sum(jnp.where(oh1, before, 0.0), axis=-1, keepdims=True)
    rank2 = jnp.sum(jnp.where(oh2, before, 0.0), axis=-1, keepdims=True)
    carry[...] = carry[...] + jnp.sum(oh, axis=0, keepdims=True)

    hi1 = jnp.floor(rank1 * (1.0 / BM))
    hi2 = jnp.floor(rank2 * (1.0 / BM))
    vals = (i1 - ROUTE_LO, i2 - ROUTE_LO, w1, w2,
            (i1 - ROUTE_LO) * NJ + hi1, (i2 - ROUTE_LO) * NJ + hi2, rank1 - hi1 * BM, rank2 - hi2 * BM)
    route = jnp.zeros_like(logits)
    for n, val in enumerate(vals):
        route = jnp.where(lane == float(n), val, route)
    route_t = route.T
    rows = [route_t[n:n + 1, part * LANES:(part + 1) * LANES] for n in (4, 5, 6, 7) for part in range(PARTS)]
    rows.append(carry[...])
    rows.append(jnp.zeros((STG_ROWS - CNT_ROW - 1, LANES), F32))
    return xn, route, jnp.concatenate(rows, axis=0).astype(I32)


def _moe_front(mixer, gffn_ref, wrh_ref, wrl_ref, route_ref, xs_hbm, dest_out, tab_out, cnt_out, nal_out,
               carry, xns, stg_v, stg_s, tab, cntp, nal, ssem, gsem, zsem):
    i = pl.program_id(0)
    n = pl.num_programs(0)

    def stage_copy(slot):
        return pltpu.make_async_copy(stg_v, stg_s.at[slot], ssem.at[slot])

    def hand_out_chunks(slot):
        def body(e, na):
            c_new = stg_s[slot, CNT_ROW, ROUTE_LO + e]
            cb = (cntp[e] + (BM - 1)) >> BM_SHIFT
            ca = (c_new + (BM - 1)) >> BM_SHIFT
            tab[e * NJ + cb] = na
            cntp[e] = c_new
            return na + (ca - cb)

        nal[0] = lax.fori_loop(0, N_EXPERTS, body, nal[0])

    def dest_row(slot, which, part, lane):
        chunk = tab[stg_s[slot, which * PARTS + part, lane]]
        return (chunk << BM_SHIFT) + stg_s[slot, (2 + which) * PARTS + part, lane]

    def scatter_row(slot, tile, which, part, lane):
        r = part * LANES + lane
        d = dest_row(slot, which, part, lane)
        dest_out[which * DEST_HALF + (tile + SCATTER_LAG) * TS + r] = d
        _row_copy(xns[slot], r, xs_hbm, d, gsem.at[slot]).start()

    def wait_scatter(slot):
        def body(r, c):
            _row_copy(xns[slot], 0, xs_hbm, 0, gsem.at[slot]).wait()
            return c

        lax.fori_loop(0, 2 * TS, body, 0, unroll=8)

    @pl.when(i == 0)
    def _():
        carry[...] = jnp.zeros_like(carry)
        for slot in range(X_SLOTS - SCATTER_LAG, X_SLOTS):
            xns[slot][...] = jnp.zeros((TS, D_MODEL), F32)
            for which in range(2):
                for part in range(PARTS):
                    def placeholder(l, c, slot=slot, which=which, part=part):
                        stg_s[slot, which * PARTS + part, l] = N_EXPERTS * NJ + which
                        stg_s[slot, (2 + which) * PARTS + part, l] = part * LANES + l
                        return c

                    lax.fori_loop(0, LANES, placeholder, 0)

        def clear_tab(j, c):
            tab[j] = 0
            return c

        lax.fori_loop(0, N_EXPERTS * NJ, clear_tab, 0)
        tab[N_EXPERTS * NJ] = N_CHUNKS
        tab[N_EXPERTS * NJ + 1] = N_CHUNKS + 1

        def clear_cnt(e, c):
            cntp[e] = 0
            return c

        lax.fori_loop(0, N_EXPERTS, clear_cnt, 0)
        nal[0] = 0

    def scatter_loop(slot, tile):
        for part in range(PARTS):
            def scatter_lane(lane, c, part=part):
                for which in range(2):
                    scatter_row(slot, tile, which, part, lane)
                return c

            lax.fori_loop(0, LANES, scatter_lane, 0)

    def variant(k):
        prev = (k + X_SLOTS - 1) % X_SLOTS
        lagged = (k + X_SLOTS - SCATTER_LAG) % X_SLOTS

        def run():
            @pl.when(i >= X_SLOTS)
            def _():
                wait_scatter(k)

            for part in range(PARTS):
                for lane in range(LANES):
                    for which in range(2):
                        scatter_row(lagged, i - SCATTER_LAG, which, part, lane)
            xn, route, stage = _route_vectors(mixer(), gffn_ref[...], wrh_ref[...], wrl_ref[...], carry)
            xns[k][...] = xn
            route_ref[...] = route

            @pl.when(i >= 1)
            def _():
                stage_copy(prev).wait()
                hand_out_chunks(prev)

            stg_v[...] = stage
            stage_copy(k).start()

        return run

    lax.switch(i % X_SLOTS, [variant(k) for k in range(X_SLOTS)])

    for step in range(SCATTER_LAG):
        @pl.when(i == step)
        def _():
            wait_scatter(X_SLOTS - SCATTER_LAG + step)

    @pl.when(i == n - 1)
    def _():
        for k in range(X_SLOTS):
            @pl.when(i % X_SLOTS == k)
            def _():
                scatter_loop((k + X_SLOTS - 1) % X_SLOTS, i - 1)
                stage_copy(k).wait()
                hand_out_chunks(k)
                scatter_loop(k, i)

        for k in range(X_SLOTS):
            wait_scatter(k)

        xns[0][...] = jnp.zeros((TS, D_MODEL), F32)

        def pad_expert(e, c):
            used = cntp[e] & (BM - 1)
            base = tab[e * NJ + (cntp[e] >> BM_SHIFT)] << BM_SHIFT
            stop = jnp.where(used > 0, BM, 0)
            lax.fori_loop(used, stop, lambda r, c2: (_row_copy(xns[0], 0, xs_hbm, base + r, zsem).start(), c2)[1], 0)
            lax.fori_loop(used, stop, lambda r, c2: (_row_copy(xns[0], 0, xs_hbm, base + r, zsem).wait(), c2)[1], 0)
            cnt_out[e] = cntp[e]
            return c

        lax.fori_loop(0, N_EXPERTS, pad_expert, 0)

        def spare(chunk):
            return pltpu.make_async_copy(xns[0], xs_hbm.at[_chunk_rows(chunk)], zsem)

        lax.fori_loop(nal[0], N_CHUNKS, lambda ch, c: (spare(ch).start(), c)[1], 0)
        lax.fori_loop(nal[0], N_CHUNKS, lambda ch, c: (spare(ch).wait(), c)[1], 0)

        def copy_tab(j, c):
            tab_out[j] = tab[j]
            return c

        lax.fori_loop(0, N_EXPERTS * NJ, copy_tab, 0)
        nal_out[0] = nal[0]


_FRONT_OUT_SHAPES = [jax.ShapeDtypeStruct((TOKENS, D_MODEL), F32),
                     jax.ShapeDtypeStruct((TOKENS, LANES), F32),
                     jax.ShapeDtypeStruct((XS_ROWS, D_MODEL), F32),
                     jax.ShapeDtypeStruct((2 * DEST_HALF,), I32),
                     jax.ShapeDtypeStruct((N_EXPERTS * NJ,), I32),
                     jax.ShapeDtypeStruct((N_EXPERTS,), I32),
                     jax.ShapeDtypeStruct((1,), I32)]


def _front_out_specs():
    tok = lambda w: pl.BlockSpec((TS, w), lambda i: (i, 0))
    smem = lambda: pl.BlockSpec(memory_space=pltpu.SMEM)
    return [tok(D_MODEL), tok(LANES), pl.BlockSpec(memory_space=pl.ANY), smem(), smem(), smem(), smem()]


def _front_scratch():
    return ([pltpu.VMEM((1, LANES), F32)] + [pltpu.VMEM((TS, D_MODEL), F32)] * X_SLOTS
            + [pltpu.VMEM((STG_ROWS, LANES), I32), pltpu.SMEM((X_SLOTS, STG_ROWS, LANES), I32),
               pltpu.SMEM((N_EXPERTS * NJ + 2,), I32), pltpu.SMEM((N_EXPERTS,), I32), pltpu.SMEM((1,), I32),
               pltpu.SemaphoreType.DMA((X_SLOTS,)), pltpu.SemaphoreType.DMA((X_SLOTS,)), pltpu.SemaphoreType.DMA])


def _split_front_scratch(scratch):
    carry = scratch[0]
    xns = tuple(scratch[1:1 + X_SLOTS])
    return (carry, xns) + tuple(scratch[1 + X_SLOTS:])


def _even_out_kernel(ya_ref, yb_ref, wo_ref, h_ref, gffn_ref, wrh_ref, wrl_ref,
                     h1_ref, route_ref, xs_hbm, dest_out, tab_out, cnt_out, nal_out, *scratch):
    def mixer():
        mix = (jnp.dot(ya_ref[...], wo_ref[0:CONV_CH, :], preferred_element_type=F32)
               + jnp.dot(yb_ref[...], wo_ref[CONV_CH:, :], preferred_element_type=F32))
        h1 = h_ref[...] + mix
        h1_ref[...] = h1
        return h1

    _moe_front(mixer, gffn_ref, wrh_ref, wrl_ref, route_ref, xs_hbm, dest_out, tab_out, cnt_out, nal_out,
               *_split_front_scratch(scratch))


def _even_out(ya, yb, wo, h, gffn, wrh, wrl):
    tok = lambda w: pl.BlockSpec((TS, w), lambda i: (i, 0))
    return pl.pallas_call(
        _even_out_kernel,
        grid=(N_TILES,),
        in_specs=[tok(CONV_CH), tok(MLA_HEADS * V_DIM), _full(wo.shape), tok(D_MODEL),
                  _full((1, D_MODEL)), _full(wrh.shape), _full(wrl.shape)],
        out_specs=_front_out_specs(),
        out_shape=_FRONT_OUT_SHAPES,
        scratch_shapes=_front_scratch(),
        compiler_params=_cparams(),
        name="even_out_route",
    )(ya, yb, wo, h, gffn, wrh, wrl)


def _gelu_tanh(x):
    return 0.5 * x * (1.0 + jnp.tanh(math.sqrt(2.0 / math.pi) * (x + 0.044715 * (x * x * x))))


def _odd_kernel(h_ref, g_ref, win_ref, vn_ref, ws_ref, bs_ref, wo_ref, gffn_ref, wrh_ref, wrl_ref,
                h1_ref, route_ref, xs_hbm, dest_out, tab_out, cnt_out, nal_out, mixbuf, *scratch):
    def mixer():
        h = h_ref[...]
        hn = _rms(h, g_ref[...]).astype(BF16)
        z = _gelu_tanh(jnp.dot(hn, win_ref[...], preferred_element_type=F32))
        u = z[:, :SG_WIDTH]
        v = _rms(z[:, SG_WIDTH:], vn_ref[...]).astype(BF16)
        r = lax.broadcasted_iota(I32, (SG_CHUNK, SG_CHUNK), 0)
        c = lax.broadcasted_iota(I32, (SG_CHUNK, SG_CHUNK), 1)
        n_chunks = TS // SG_CHUNK
        dg = SG_WIDTH // SG_GROUPS
        for gi in range(SG_GROUPS):
            gs = slice(gi * dg, (gi + 1) * dg)
            wtril = jnp.where(c <= r, ws_ref[gi], 0.0).astype(BF16)
            rhs = jnp.concatenate([v[ci * SG_CHUNK:(ci + 1) * SG_CHUNK, gs] for ci in range(n_chunks)], axis=1)
            sg = jnp.dot(wtril, rhs, preferred_element_type=F32)
            for ci in range(n_chunks):
                rs = slice(ci * SG_CHUNK, (ci + 1) * SG_CHUNK)
                gate = sg[:, ci * dg:(ci + 1) * dg] + bs_ref[:, gs]
                mixbuf[rs, gs] = (u[rs, gs] * gate).astype(BF16)
        h1 = h + jnp.dot(mixbuf[...], wo_ref[...], preferred_element_type=F32)
        h1_ref[...] = h1
        return h1

    _moe_front(mixer, gffn_ref, wrh_ref, wrl_ref, route_ref, xs_hbm, dest_out, tab_out, cnt_out, nal_out,
               *_split_front_scratch(scratch))


def _odd_layer(h, g, win, vn, ws, bs_full, wo, gffn, wrh, wrl):
    tok = lambda w: pl.BlockSpec((TS, w), lambda i: (i, 0))
    return pl.pallas_call(
        _odd_kernel,
        grid=(N_TILES,),
        in_specs=[tok(D_MODEL), _full((1, D_MODEL)), _full(win.shape), _full((1, SG_WIDTH)),
                  _full(ws.shape), _full(bs_full.shape), _full(wo.shape),
                  _full((1, D_MODEL)), _full(wrh.shape), _full(wrl.shape)],
        out_specs=_front_out_specs(),
        out_shape=_FRONT_OUT_SHAPES,
        scratch_shapes=[pltpu.VMEM((TS, SG_WIDTH), BF16)] + _front_scratch(),
        compiler_params=_cparams(),
        name="odd_mixer_route",
    )(h, g, win, vn, ws, bs_full, wo, gffn, wrh, wrl)


def _gmm_kernel(tab_ref, cnt_ref, nal_ref, wg_ref, wu_ref, wd_ref, xs_hbm, yb_hbm,
                wgb, wub, wdb, xbuf, ybuf, order, first, xsem, ysem, zsem):
    e = pl.program_id(0)
    nblk = nal_ref[0]

    @pl.when(e == 0)
    def _():
        def list_expert(ex, g):
            first[ex] = g
            nb = (cnt_ref[ex] + (BM - 1)) >> BM_SHIFT

            def put(j, c):
                order[g + j] = tab_ref[ex * NJ + j]
                return c

            lax.fori_loop(0, nb, put, 0)
            return g + nb

        first[N_EXPERTS] = lax.fori_loop(0, N_EXPERTS, list_expert, 0)

    def x_copy(g, slot):
        return pltpu.make_async_copy(xs_hbm.at[_chunk_rows(order[g])], xbuf.at[slot], xsem.at[slot])

    def y_copy(g, slot):
        return pltpu.make_async_copy(ybuf.at[slot], yb_hbm.at[_chunk_rows(order[g])], ysem.at[slot])

    @pl.when(e == 0)
    def _():
        for a in range(AHEAD):
            @pl.when(a < nblk)
            def _():
                x_copy(a, a).start()

    g0 = first[e]
    g1 = first[e + 1]

    @pl.when(g1 > g0)
    def _():
        wgb[...] = wg_ref[0].astype(BF16)
        wub[...] = wu_ref[0].astype(BF16)
        wdb[...] = wd_ref[0].astype(BF16)

        def block(g, c):
            slot = g % X_SLOTS
            ys = g & 1

            @pl.when(g + AHEAD < nblk)
            def _():
                x_copy(g + AHEAD, (g + AHEAD) % X_SLOTS).start()

            x_copy(g, slot).wait()

            @pl.when(g >= 2)
            def _():
                y_copy(g - 2, ys).wait()

            x = xbuf[slot].astype(BF16)
            gate = jnp.dot(x, wgb[...], preferred_element_type=F32)
            up = jnp.dot(x, wub[...], preferred_element_type=F32)
            mid = (gate * jax.nn.sigmoid(gate) * up).astype(BF16)
            ybuf[ys] = jnp.dot(mid, wdb[...], preferred_element_type=F32)
            y_copy(g, ys).start()
            return c

        lax.fori_loop(g0, g1, block, 0)

    @pl.when(e == pl.num_programs(0) - 1)
    def _():
        @pl.when(nblk >= 2)
        def _():
            y_copy(nblk - 2, nblk & 1).wait()

        y_copy(nblk - 1, (nblk - 1) & 1).wait()
        xbuf[0] = jnp.zeros((BM, D_MODEL), F32)

        def spare(chunk):
            return pltpu.make_async_copy(xbuf.at[0], yb_hbm.at[_chunk_rows(chunk)], zsem)

        lax.fori_loop(nblk, N_CHUNKS, lambda ch, c: (spare(ch).start(), c)[1], 0)
        lax.fori_loop(nblk, N_CHUNKS, lambda ch, c: (spare(ch).wait(), c)[1], 0)


def _gmm(tab, cnt, nal, xs, w_gate, w_up, w_down, layer):
    wsel = lambda e, *_: (layer * N_EXPERTS + e, 0, 0)
    hbm = pl.BlockSpec(memory_space=pl.ANY)
    return pl.pallas_call(
        _gmm_kernel,
        grid_spec=pltpu.PrefetchScalarGridSpec(
            num_scalar_prefetch=3,
            grid=(N_EXPERTS,),
            in_specs=[pl.BlockSpec((1, D_MODEL, D_EXPERT), wsel),
                      pl.BlockSpec((1, D_MODEL, D_EXPERT), wsel),
                      pl.BlockSpec((1, D_EXPERT, D_MODEL), wsel),
                      hbm],
            out_specs=hbm,
            scratch_shapes=[pltpu.VMEM((D_MODEL, D_EXPERT), BF16), pltpu.VMEM((D_MODEL, D_EXPERT), BF16),
                            pltpu.VMEM((D_EXPERT, D_MODEL), BF16),
                            pltpu.VMEM((X_SLOTS, BM, D_MODEL), F32), pltpu.VMEM((2, BM, D_MODEL), F32),
                            pltpu.SMEM((N_CHUNKS,), I32), pltpu.SMEM((N_EXPERTS + 1,), I32),
                            pltpu.SemaphoreType.DMA((X_SLOTS,)), pltpu.SemaphoreType.DMA((2,)),
                            pltpu.SemaphoreType.DMA],
        ),
        out_shape=jax.ShapeDtypeStruct((N_CHUNKS * BM, D_MODEL), F32),
        compiler_params=_cparams(),
        name="moe_grouped_matmul",
    )(tab, cnt, nal, w_gate, w_up, w_down, xs)


def _combine_ple_kernel(dest_ref, h1_ref, route_ref, p_ref, gple_ref, wgate_ref, wproj_ref, fin_ref,
                        yb_hbm, out_ref, buf0, buf1, buf2, buf3, sem, *, final):
    i = pl.program_id(0)
    n = pl.num_programs(0)
    bufs = (buf0, buf1, buf2, buf3)

    def copies(tile, r, slot):
        t = (tile + SCATTER_LAG) * TS + r
        return [_row_copy(yb_hbm, dest_ref[which * DEST_HALF + t], bufs[slot], which * TS + r, sem.at[slot])
                for which in range(2)]

    def issue_loop(tile, slot):
        def body(r, c):
            for cp in copies(tile, r, slot):
                cp.start()
            return c

        lax.fori_loop(0, TS, body, 0)

    def issue_unrolled(tile, slot):
        for r in range(TS):
            for cp in copies(tile, r, slot):
                cp.start()

    def wait_tile(slot):
        def body(r, c):
            _row_copy(yb_hbm, 0, bufs[slot], 0, sem.at[slot]).wait()
            return c

        lax.fori_loop(0, 2 * TS, body, 0, unroll=8)

    def compute(quarter, slot):
        rows = slice(quarter * TS, (quarter + 1) * TS)
        route = route_ref[rows, :]
        w1 = route[:, 2:3]
        w2 = route[:, 3:4]
        h2 = h1_ref[rows, :] + (w1 * bufs[slot][0:TS, :] + w2 * bufs[slot][TS:2 * TS, :])
        gate = jax.nn.sigmoid(jnp.dot(_rms(h2, gple_ref[...]).astype(BF16), wgate_ref[...],
                                      preferred_element_type=F32))
        out = h2 + gate * jnp.dot(p_ref[rows, :].astype(BF16), wproj_ref[...], preferred_element_type=F32)
        if final:
            out = _rms(out, fin_ref[...])
        out_ref[rows, :] = out

    @pl.when(i == 0)
    def _():
        for k in range(AHEAD):
            issue_loop(k, k)

    for k in range(X_SLOTS):
        tile = X_SLOTS * i + k
        ahead = jnp.minimum(tile + AHEAD, N_TILES - 1)
        wait_tile(k)
        issue_unrolled(ahead, (k + AHEAD) % X_SLOTS)
        compute(k, k)

    @pl.when(i == n - 1)
    def _():
        for k in range(AHEAD):
            wait_tile((N_TILES + k) % X_SLOTS)


def _combine_ple(dest, h1, route, p_all, layer, gple, wgate, wproj, fin, yb, final):
    nt = TOKENS // (X_SLOTS * TS)
    tok = lambda w: pl.BlockSpec((X_SLOTS * TS, w), lambda i, *_: (i, 0))
    full = lambda shape: pl.BlockSpec(shape, lambda i, *_: (0,) * len(shape))
    return pl.pallas_call(
        functools.partial(_combine_ple_kernel, final=final),
        grid_spec=pltpu.PrefetchScalarGridSpec(
            num_scalar_prefetch=1,
            grid=(nt,),
            in_specs=[tok(D_MODEL), tok(LANES),
                      pl.BlockSpec((X_SLOTS * TS, PLE_DIM), lambda i, *_: (layer * nt + i, 0)),
                      full((1, D_MODEL)), full(wgate.shape), full(wproj.shape), full((1, D_MODEL)),
                      pl.BlockSpec(memory_space=pl.ANY)],
            out_specs=tok(D_MODEL),
            scratch_shapes=[pltpu.VMEM((2 * TS, D_MODEL), F32)] * X_SLOTS + [pltpu.SemaphoreType.DMA((X_SLOTS,))],
        ),
        out_shape=jax.ShapeDtypeStruct((TOKENS, D_MODEL), F32),
        compiler_params=_cparams(),
        name="moe_combine_ple",
    )(dest, h1, route, p_all, gple, wgate, wproj, fin, yb)


def _router_weights(w_rg, w_re):
    w = jnp.zeros((D_MODEL, LANES), F32).at[:, :N_GROUPS].set(w_rg).at[:, ROUTE_LO:ROUTE_LO + N_EXPERTS].set(w_re)
    hi = w.astype(BF16)
    lo = (w - hi.astype(F32)).astype(BF16)
    return hi, lo


def _even_weights(w_in, w_q_up, w_kv_up):
    base = 3 * CONV_CH + Q_RANK + KV_RANK
    kpe_cols = jnp.zeros((D_MODEL, HEAD_PAD), F32).at[:, ROPE_LO:ROPE_LO + QK_ROPE].set(w_in[:, base:])
    win = jnp.concatenate([w_in[:, :base], kpe_cols], axis=1).astype(BF16)
    dqk = QK_NOPE + QK_ROPE
    wq = jnp.pad(w_q_up.reshape(Q_RANK, MLA_HEADS, dqk), ((0, 0), (0, 0), (0, HEAD_PAD - dqk)))
    wq = wq.reshape(Q_RANK, MLA_HEADS * HEAD_PAD).astype(BF16)
    wkv = w_kv_up.reshape(KV_RANK, MLA_HEADS, QK_NOPE + V_DIM)
    wk = jnp.pad(wkv[:, :, :QK_NOPE], ((0, 0), (0, 0), (0, HEAD_PAD - QK_NOPE))).reshape(KV_RANK, MLA_HEADS * HEAD_PAD)
    wv = wkv[:, :, QK_NOPE:].reshape(KV_RANK, MLA_HEADS * V_DIM)
    return win, wq, jnp.concatenate([wk, wv], axis=1).astype(BF16)


def kernel(x, p, positions, norm_mix, norm_ffn, w_in_e, conv_w, q_norm, w_q_up, kv_norm, w_kv_up, w_out_e,
           w_in_o, v_norm, w_s, b_s, w_out_o, w_router_group, w_router_expert, w_gate, w_up, w_down,
           norm_ple, w_ple_gate, w_ple_proj, final_norm):
    h = x.reshape(TOKENS, D_MODEL)
    p_all = p.reshape(DEPTH * TOKENS, PLE_DIM)
    wg_all = w_gate.reshape(DEPTH * N_EXPERTS, D_MODEL, D_EXPERT)
    wu_all = w_up.reshape(DEPTH * N_EXPERTS, D_MODEL, D_EXPERT)
    wd_all = w_down.reshape(DEPTH * N_EXPERTS, D_EXPERT, D_MODEL)
    ctab, stab = _rope_tables(positions)
    row = lambda a: a.reshape(1, -1)

    for i in range(DEPTH):
        j = i // 2
        wrh, wrl = _router_weights(w_router_group[i], w_router_expert[i])
        if i % 2 == 0:
            win, wq, wkv = _even_weights(w_in_e[j], w_q_up[j], w_kv_up[j])
            ya, q, k, v = _even_front(h, row(norm_mix[i]), win, conv_w[j], row(q_norm[j]), wq,
                                      row(kv_norm[j]), wkv, ctab, stab)
            yb = _attention(q, k, v)
            front = _even_out(ya, yb, w_out_e[j].astype(BF16), h, row(norm_ffn[i]), wrh, wrl)
        else:
            bs_full = jnp.repeat(b_s[j].T, SG_WIDTH // SG_GROUPS, axis=1)
            front = _odd_layer(h, row(norm_mix[i]), w_in_o[j].astype(BF16), row(v_norm[j]), w_s[j], bs_full,
                               w_out_o[j].astype(BF16), row(norm_ffn[i]), wrh, wrl)
        h1, route, xs, dest, tab, cnt, nal = front
        ys = _gmm(tab, cnt, nal, xs, wg_all, wu_all, wd_all, i)
        h = _combine_ple(dest, h1, route, p_all, i, row(norm_ple[i]), w_ple_gate[i].astype(BF16),
                         w_ple_proj[i].astype(BF16), row(final_norm), ys, final=(i == DEPTH - 1))
    return h.reshape(BATCH, SEQ, D_MODEL)
```

```python
import functools
import math

import jax
import jax.numpy as jnp
from jax import lax
from jax.experimental import pallas as pl
from jax.experimental.pallas import tpu as pltpu

F32 = jnp.float32
BF16 = jnp.bfloat16
I32 = jnp.int32

D_MODEL = 1024
BATCH = 4
SEQ = 4096
DEPTH = 4
TOKENS = BATCH * SEQ

CONV_CH = 512
MLA_HEADS = 8
QK_NOPE = 64
QK_ROPE = 32
V_DIM = 64
Q_RANK = 384
KV_RANK = 256
ROPE_THETA = 10000.0
SG_WIDTH = 1024
SG_GROUPS = 8
SG_CHUNK = 128
N_GROUPS = 4
EXP_PER_GROUP = 8
N_EXPERTS = N_GROUPS * EXP_PER_GROUP
D_EXPERT = 512
PLE_DIM = 256
NORM_EPS = 1e-6
NEG_INF = -1e30

LANES = 128
HEAD_PAD = LANES
ROPE_LO = QK_NOPE
ROPE_HALF = QK_ROPE // 2

TS = 256
PARTS = TS // LANES
TQ = 1024
BM = 256
BM_SHIFT = BM.bit_length() - 1
assert 1 << BM_SHIFT == BM and TS == BM
N_TILES = TOKENS // TS
N_CHUNKS = 2 * TOKENS // BM + N_EXPERTS
NJ = TOKENS // BM + 1
XS_ROWS = (N_CHUNKS + 2) * BM
SCATTER_LAG = 2
DEST_HALF = TOKENS + SCATTER_LAG * TS
ROUTE_LO = N_GROUPS
STG_ROWS = 16
CNT_ROW = 4 * PARTS
X_SLOTS = 4
AHEAD = 2

VMEM_LIMIT = 48 * 1024 * 1024

Q_SCALE = math.log2(math.e) / math.sqrt(QK_NOPE + QK_ROPE)


def _cparams(n_axes=1):
    return pltpu.CompilerParams(dimension_semantics=("arbitrary",) * n_axes,
                                vmem_limit_bytes=VMEM_LIMIT)


def _rms(x, g):
    return x * lax.rsqrt(jnp.mean(x * x, axis=-1, keepdims=True) + NORM_EPS) * g


def _full(shape):
    return pl.BlockSpec(shape, lambda *_: (0,) * len(shape))


def _row_copy(src, si, dst, di, sem):
    return pltpu.make_async_copy(src.at[pl.ds(si, 1)], dst.at[pl.ds(di, 1)], sem)


def _chunk_rows(chunk):
    return pl.ds(pl.multiple_of(chunk * BM, BM), BM)


def _rope_table_kernel(pos_ref, freq_ref, c_ref, s_ref):
    ang = pos_ref[...].astype(F32) * freq_ref[...]
    lane = lax.broadcasted_iota(I32, ang.shape, 1)
    cosv = jnp.cos(ang)
    sinv = jnp.sin(ang)
    in_rope = (lane >= ROPE_LO) & (lane < ROPE_LO + QK_ROPE)
    first_half = lane < ROPE_LO + ROPE_HALF
    c_ref[...] = jnp.where(lane < ROPE_LO, 1.0, jnp.where(in_rope, cosv, 0.0))
    s_ref[...] = jnp.where(in_rope, jnp.where(first_half, -sinv, sinv), 0.0)


def _rope_tables(positions):
    inv_freq = 1.0 / (ROPE_THETA ** (jnp.arange(0, QK_ROPE, 2, dtype=F32) / QK_ROPE))
    freq = jnp.zeros((LANES,), F32).at[ROPE_LO:ROPE_LO + QK_ROPE].set(jnp.tile(inv_freq, 2))
    pos = positions.reshape(TOKENS, 1)
    return pl.pallas_call(
        _rope_table_kernel,
        grid=(TOKENS // TS,),
        in_specs=[pl.BlockSpec((TS, 1), lambda i: (i, 0)), _full((1, LANES))],
        out_specs=[pl.BlockSpec((TS, LANES), lambda i: (i, 0))] * 2,
        out_shape=[jax.ShapeDtypeStruct((TOKENS, LANES), F32)] * 2,
        compiler_params=_cparams(),
        name="rope_tables",
    )(pos, freq.reshape(1, LANES))


def _rope(x, c, s, first_half):
    partner = jnp.where(first_half, pltpu.roll(x, LANES - ROPE_HALF, 1), pltpu.roll(x, ROPE_HALF, 1))
    return x * c + partner * s


def _even_front_kernel(h_ref, g_ref, win_ref, cw_ref, qn_ref, wq_ref, kvn_ref, wkv_ref, c_ref, s_ref,
                       ya_ref, q_ref, k_ref, v_ref, zbuf):
    i = pl.program_id(0)
    hn = _rms(h_ref[...], g_ref[...]).astype(BF16)
    proj = jnp.dot(hn, win_ref[...], preferred_element_type=F32)
    xc = proj[:, 0:CONV_CH]
    gb = proj[:, CONV_CH:2 * CONV_CH]
    gc = proj[:, 2 * CONV_CH:3 * CONV_CH]
    o = 3 * CONV_CH
    cq = proj[:, o:o + Q_RANK]
    ckv = proj[:, o + Q_RANK:o + Q_RANK + KV_RANK]
    kpe = proj[:, o + Q_RANK + KV_RANK:]

    @pl.when(i % (SEQ // TS) == 0)
    def _():
        zbuf[0:8, :] = jnp.zeros((8, CONV_CH), F32)

    z = gc * xc
    zbuf[8:8 + TS, :] = z
    z1 = zbuf[7:7 + TS, :]
    z2 = zbuf[6:6 + TS, :]
    cw = cw_ref[...]
    conv = cw[0:1, :] * z2 + cw[1:2, :] * z1 + cw[2:3, :] * z
    ya_ref[...] = (gb * conv).astype(BF16)
    zbuf[0:8, :] = zbuf[TS:TS + 8, :]

    c = c_ref[...]
    s = s_ref[...]
    lane = lax.broadcasted_iota(I32, (TS, LANES), 1)
    first_half = lane < ROPE_LO + ROPE_HALF

    cqn = _rms(cq, qn_ref[...]).astype(BF16)
    q = jnp.dot(cqn, wq_ref[...], preferred_element_type=F32)
    for hd in range(MLA_HEADS):
        sl = slice(hd * HEAD_PAD, (hd + 1) * HEAD_PAD)
        q_ref[:, sl] = (_rope(q[:, sl], c, s, first_half) * Q_SCALE).astype(BF16)

    ckvn = _rms(ckv, kvn_ref[...]).astype(BF16)
    kv = jnp.dot(ckvn, wkv_ref[...], preferred_element_type=F32)
    kper = _rope(kpe, c, s, first_half)
    for hd in range(MLA_HEADS):
        sl = slice(hd * HEAD_PAD, (hd + 1) * HEAD_PAD)
        k_ref[:, sl] = (kv[:, sl] + kper).astype(BF16)
    v_ref[...] = kv[:, MLA_HEADS * HEAD_PAD:].astype(BF16)


def _even_front(h, g, win, cw, qn, wq, kvn, wkv, ctab, stab):
    tok = lambda w: pl.BlockSpec((TS, w), lambda i: (i, 0))
    return pl.pallas_call(
        _even_front_kernel,
        grid=(TOKENS // TS,),
        in_specs=[tok(D_MODEL), _full((1, D_MODEL)), _full(win.shape), _full(cw.shape),
                  _full((1, Q_RANK)), _full(wq.shape), _full((1, KV_RANK)), _full(wkv.shape),
                  tok(LANES), tok(LANES)],
        out_specs=[tok(CONV_CH), tok(MLA_HEADS * HEAD_PAD), tok(MLA_HEADS * HEAD_PAD), tok(MLA_HEADS * V_DIM)],
        out_shape=[jax.ShapeDtypeStruct((TOKENS, CONV_CH), BF16),
                   jax.ShapeDtypeStruct((TOKENS, MLA_HEADS * HEAD_PAD), BF16),
                   jax.ShapeDtypeStruct((TOKENS, MLA_HEADS * HEAD_PAD), BF16),
                   jax.ShapeDtypeStruct((TOKENS, MLA_HEADS * V_DIM), BF16)],
        scratch_shapes=[pltpu.VMEM((TS + 8, CONV_CH), F32)],
        compiler_params=_cparams(),
        name="even_front",
    )(h, g, win, cw, qn, wq, kvn, wkv, ctab, stab)


def _attn_kernel(q_ref, k_ref, v_ref, o_ref):
    i = pl.program_id(2)
    nt = (((1,), (1,)), ((), ()))
    for hh in range(2):
        hs = slice(hh * HEAD_PAD, (hh + 1) * HEAD_PAD)
        q = q_ref[:, hs]

        def step(j, carry):
            m, l, acc = carry
            kb = k_ref[pl.ds(pl.multiple_of(j * TQ, TQ), TQ), hs]
            vb = v_ref[pl.ds(pl.multiple_of(j * TQ, TQ), TQ), :]
            s = lax.dot_general(q, kb, nt, preferred_element_type=F32)
            m_new = jnp.maximum(m, jnp.max(s, axis=-1, keepdims=True))
            alpha = jnp.exp2(m - m_new)
            p = jnp.exp2(s - m_new)
            l = alpha * l + jnp.sum(p, axis=-1, keepdims=True)
            acc = alpha * acc + jnp.dot(p.astype(BF16), vb, preferred_element_type=F32)
            return m_new, l, acc

        init = (jnp.full((TQ, 1), NEG_INF, F32), jnp.zeros((TQ, 1), F32), jnp.zeros((TQ, 2 * V_DIM), F32))
        m, l, acc = lax.fori_loop(0, i, step, init)
        vs = slice(hh * V_DIM, (hh + 1) * V_DIM)
        base = pl.multiple_of(i * TQ, TQ)
        half_rows = TQ // 2
        for half in range(2):
            rs = slice(half * half_rows, (half + 1) * half_rows)
            nk = (half + 1) * half_rows
            kb = k_ref[pl.ds(base, nk), hs]
            vb = v_ref[pl.ds(base, nk), :]
            s = lax.dot_general(q[rs, :], kb, nt, preferred_element_type=F32)
            qpos = lax.broadcasted_iota(I32, (half_rows, nk), 0) + half * half_rows
            kpos = lax.broadcasted_iota(I32, (half_rows, nk), 1)
            s = jnp.where(kpos <= qpos, s, NEG_INF)
            m_new = jnp.maximum(m[rs, :], jnp.max(s, axis=-1, keepdims=True))
            alpha = jnp.exp2(m[rs, :] - m_new)
            p = jnp.exp2(s - m_new)
            l_new = alpha * l[rs, :] + jnp.sum(p, axis=-1, keepdims=True)
            acc_new = alpha * acc[rs, :] + jnp.dot(p.astype(BF16), vb, preferred_element_type=F32)
            o_ref[rs, vs] = (acc_new / l_new)[:, vs].astype(BF16)


def _attention(q, k, v):
    nq = SEQ // TQ
    pairs = MLA_HEADS // 2
    return pl.pallas_call(
        _attn_kernel,
        grid=(BATCH, pairs, nq),
        in_specs=[pl.BlockSpec((TQ, 2 * HEAD_PAD), lambda b, p, i: (b * nq + i, p)),
                  pl.BlockSpec((SEQ, 2 * HEAD_PAD), lambda b, p, i: (b, p)),
                  pl.BlockSpec((SEQ, 2 * V_DIM), lambda b, p, i: (b, p))],
        out_specs=pl.BlockSpec((TQ, 2 * V_DIM), lambda b, p, i: (b * nq + i, p)),
        out_shape=jax.ShapeDtypeStruct((TOKENS, MLA_HEADS * V_DIM), BF16),
        compiler_params=_cparams(3),
        name="mla_attention",
    )(q, k, v)


def _route_vectors(h1, gffn, wr_hi, wr_lo, carry):
    xn = _rms(h1, gffn)
    x_hi = xn.astype(BF16)
    x_lo = (xn - x_hi.astype(F32)).astype(BF16)
    logits = (jnp.dot(x_hi, wr_hi, preferred_element_type=F32)
              + jnp.dot(x_lo, wr_hi, preferred_element_type=F32)
              + jnp.dot(x_hi, wr_lo, preferred_element_type=F32))
    lane = lax.broadcasted_iota(I32, logits.shape, 1).astype(F32)
    big = float(LANES)

    is_g = lane < N_GROUPS
    gl = jnp.where(is_g, logits, NEG_INF)
    gmax = jnp.max(gl, axis=-1, keepdims=True)
    gidx = jnp.min(jnp.where(gl == gmax, lane, big), axis=-1, keepdims=True)
    gsum = jnp.sum(jnp.where(is_g, jnp.exp(gl - gmax), 0.0), axis=-1, keepdims=True)
    g_w = 1.0 / gsum

    lo = ROUTE_LO + gidx * EXP_PER_GROUP
    el = jnp.where((lane >= lo) & (lane < lo + EXP_PER_GROUP), logits, NEG_INF)
    m1 = jnp.max(el, axis=-1, keepdims=True)
    i1 = jnp.min(jnp.where(el == m1, lane, big), axis=-1, keepdims=True)
    el2 = jnp.where(lane == i1, NEG_INF, el)
    m2 = jnp.max(el2, axis=-1, keepdims=True)
    i2 = jnp.min(jnp.where(el2 == m2, lane, big), axis=-1, keepdims=True)
    t = jnp.exp(m2 - m1)
    w1 = g_w / (1.0 + t)
    w2 = g_w * t / (1.0 + t)

    oh1 = lane == i1
    oh2 = lane == i2
    oh = jnp.where(oh1 | oh2, 1.0, 0.0)
    r = lax.broadcasted_iota(I32, (TS, TS), 0)
    cc = lax.broadcasted_iota(I32, (TS, TS), 1)
    tri = jnp.where(cc < r, 1.0, 0.0).astype(BF16)
    before = jnp.dot(tri, oh.astype(BF16), preferred_element_type=F32) + carry[...]
    rank1 = jnp.sum(jnp.where(oh1, before, 0.0), axis=-1, keepdims=True)
    rank2 = jnp.sum(jnp.where(oh2, before, 0.0), axis=-1, keepdims=True)
    carry[...] = carry[...] + jnp.sum(oh, axis=0, keepdims=True)

    hi1 = jnp.floor(rank1 * (1.0 / BM))
    hi2 = jnp.floor(rank2 * (1.0 / BM))
    vals = (i1 - ROUTE_LO, i2 - ROUTE_LO, w1, w2,
            (i1 - ROUTE_LO) * NJ + hi1, (i2 - ROUTE_LO) * NJ + hi2, rank1 - hi1 * BM, rank2 - hi2 * BM)
    route = jnp.zeros_like(logits)
    for n, val in enumerate(vals):
        route = jnp.where(lane == float(n), val, route)
    route_t = route.T
    rows = [route_t[n:n + 1, part * LANES:(part + 1) * LANES] for n in (4, 5, 6, 7) for part in range(PARTS)]
    rows.append(carry[...])
    rows.append(jnp.zeros((STG_ROWS - CNT_ROW - 1, LANES), F32))
    return xn, route, jnp.concatenate(rows, axis=0).astype(I32)


def _moe_front(mixer, gffn_ref, wrh_ref, wrl_ref, route_ref, xs_hbm, dest_out, tab_out, cnt_out, nal_out,
               carry, xns, stg_v, stg_s, tab, cntp, nal, ssem, gsem, zsem):
    i = pl.program_id(0)
    n = pl.num_programs(0)

    def stage_copy(slot):
        return pltpu.make_async_copy(stg_v, stg_s.at[slot], ssem.at[slot])

    def hand_out_chunks(slot):
        def body(e, na):
            c_new = stg_s[slot, CNT_ROW, ROUTE_LO + e]
            cb = (cntp[e] + (BM - 1)) >> BM_SHIFT
            ca = (c_new + (BM - 1)) >> BM_SHIFT
            tab[e * NJ + cb] = na
            cntp[e] = c_new
            return na + (ca - cb)

        nal[0] = lax.fori_loop(0, N_EXPERTS, body, nal[0])

    def dest_row(slot, which, part, lane):
        chunk = tab[stg_s[slot, which * PARTS + part, lane]]
        return (chunk << BM_SHIFT) + stg_s[slot, (2 + which) * PARTS + part, lane]

    def scatter_row(slot, tile, which, part, lane):
        r = part * LANES + lane
        d = dest_row(slot, which, part, lane)
        dest_out[which * DEST_HALF + (tile + SCATTER_LAG) * TS + r] = d
        _row_copy(xns[slot], r, xs_hbm, d, gsem.at[slot]).start(priority=which)

    def wait_scatter(slot):
        def body(r, c):
            _row_copy(xns[slot], 0, xs_hbm, 0, gsem.at[slot]).wait()
            return c

        lax.fori_loop(0, 2 * TS, body, 0, unroll=8)

    @pl.when(i == 0)
    def _():
        carry[...] = jnp.zeros_like(carry)
        for slot in range(X_SLOTS - SCATTER_LAG, X_SLOTS):
            xns[slot][...] = jnp.zeros((TS, D_MODEL), F32)
            for which in range(2):
                for part in range(PARTS):
                    def placeholder(l, c, slot=slot, which=which, part=part):
                        stg_s[slot, which * PARTS + part, l] = N_EXPERTS * NJ + which
                        stg_s[slot, (2 + which) * PARTS + part, l] = part * LANES + l
                        return c

                    lax.fori_loop(0, LANES, placeholder, 0)

        def clear_tab(j, c):
            tab[j] = 0
            return c

        lax.fori_loop(0, N_EXPERTS * NJ, clear_tab, 0)
        tab[N_EXPERTS * NJ] = N_CHUNKS
        tab[N_EXPERTS * NJ + 1] = N_CHUNKS + 1

        def clear_cnt(e, c):
            cntp[e] = 0
            return c

        lax.fori_loop(0, N_EXPERTS, clear_cnt, 0)
        nal[0] = 0

    def scatter_loop(slot, tile):
        for part in range(PARTS):
            def scatter_lane(lane, c, part=part):
                for which in range(2):
                    scatter_row(slot, tile, which, part, lane)
                return c

            lax.fori_loop(0, LANES, scatter_lane, 0)

    def variant(k):
        prev = (k + X_SLOTS - 1) % X_SLOTS
        lagged = (k + X_SLOTS - SCATTER_LAG) % X_SLOTS

        def run():
            @pl.when(i >= X_SLOTS)
            def _():
                wait_scatter(k)

            for part in range(PARTS):
                for lane in range(LANES):
                    for which in range(2):
                        scatter_row(lagged, i - SCATTER_LAG, which, part, lane)
            xn, route, stage = _route_vectors(mixer(), gffn_ref[...], wrh_ref[...], wrl_ref[...], carry)
            xns[k][...] = xn
            route_ref[...] = route

            @pl.when(i >= 1)
            def _():
                stage_copy(prev).wait()
                hand_out_chunks(prev)

            stg_v[...] = stage
            stage_copy(k).start()

        return run

    lax.switch(i % X_SLOTS, [variant(k) for k in range(X_SLOTS)])

    for step in range(SCATTER_LAG):
        @pl.when(i == step)
        def _():
            wait_scatter(X_SLOTS - SCATTER_LAG + step)

    @pl.when(i == n - 1)
    def _():
        for k in range(X_SLOTS):
            @pl.when(i % X_SLOTS == k)
            def _():
                scatter_loop((k + X_SLOTS - 1) % X_SLOTS, i - 1)
                stage_copy(k).wait()
                hand_out_chunks(k)
                scatter_loop(k, i)

        for k in range(X_SLOTS):
            wait_scatter(k)

        xns[0][...] = jnp.zeros((TS, D_MODEL), F32)

        def pad_expert(e, c):
            used = cntp[e] & (BM - 1)
            base = tab[e * NJ + (cntp[e] >> BM_SHIFT)] << BM_SHIFT
            stop = jnp.where(used > 0, BM, 0)
            lax.fori_loop(used, stop, lambda r, c2: (_row_copy(xns[0], 0, xs_hbm, base + r, zsem).start(), c2)[1], 0)
            lax.fori_loop(used, stop, lambda r, c2: (_row_copy(xns[0], 0, xs_hbm, base + r, zsem).wait(), c2)[1], 0)
            cnt_out[e] = cntp[e]
            return c

        lax.fori_loop(0, N_EXPERTS, pad_expert, 0)

        def spare(chunk):
            return pltpu.make_async_copy(xns[0], xs_hbm.at[_chunk_rows(chunk)], zsem)

        lax.fori_loop(nal[0], N_CHUNKS, lambda ch, c: (spare(ch).start(), c)[1], 0)
        lax.fori_loop(nal[0], N_CHUNKS, lambda ch, c: (spare(ch).wait(), c)[1], 0)

        def copy_tab(j, c):
            tab_out[j] = tab[j]
            return c

        lax.fori_loop(0, N_EXPERTS * NJ, copy_tab, 0)
        nal_out[0] = nal[0]


_FRONT_OUT_SHAPES = [jax.ShapeDtypeStruct((TOKENS, D_MODEL), F32),
                     jax.ShapeDtypeStruct((TOKENS, LANES), F32),
                     jax.ShapeDtypeStruct((XS_ROWS, D_MODEL), F32),
                     jax.ShapeDtypeStruct((2 * DEST_HALF,), I32),
                     jax.ShapeDtypeStruct((N_EXPERTS * NJ,), I32),
                     jax.ShapeDtypeStruct((N_EXPERTS,), I32),
                     jax.ShapeDtypeStruct((1,), I32)]


def _front_out_specs():
    tok = lambda w: pl.BlockSpec((TS, w), lambda i: (i, 0))
    smem = lambda: pl.BlockSpec(memory_space=pltpu.SMEM)
    return [tok(D_MODEL), tok(LANES), pl.BlockSpec(memory_space=pl.ANY), smem(), smem(), smem(), smem()]


def _front_scratch():
    return ([pltpu.VMEM((1, LANES), F32)] + [pltpu.VMEM((TS, D_MODEL), F32)] * X_SLOTS
            + [pltpu.VMEM((STG_ROWS, LANES), I32), pltpu.SMEM((X_SLOTS, STG_ROWS, LANES), I32),
               pltpu.SMEM((N_EXPERTS * NJ + 2,), I32), pltpu.SMEM((N_EXPERTS,), I32), pltpu.SMEM((1,), I32),
               pltpu.SemaphoreType.DMA((X_SLOTS,)), pltpu.SemaphoreType.DMA((X_SLOTS,)), pltpu.SemaphoreType.DMA])


def _split_front_scratch(scratch):
    carry = scratch[0]
    xns = tuple(scratch[1:1 + X_SLOTS])
    return (carry, xns) + tuple(scratch[1 + X_SLOTS:])


def _even_out_kernel(ya_ref, yb_ref, wo_ref, h_ref, gffn_ref, wrh_ref, wrl_ref,
                     h1_ref, route_ref, xs_hbm, dest_out, tab_out, cnt_out, nal_out, *scratch):
    def mixer():
        mix = (jnp.dot(ya_ref[...], wo_ref[0:CONV_CH, :], preferred_element_type=F32)
               + jnp.dot(yb_ref[...], wo_ref[CONV_CH:, :], preferred_element_type=F32))
        h1 = h_ref[...] + mix
        h1_ref[...] = h1
        return h1

    _moe_front(mixer, gffn_ref, wrh_ref, wrl_ref, route_ref, xs_hbm, dest_out, tab_out, cnt_out, nal_out,
               *_split_front_scratch(scratch))


def _even_out(ya, yb, wo, h, gffn, wrh, wrl):
    tok = lambda w: pl.BlockSpec((TS, w), lambda i: (i, 0))
    return pl.pallas_call(
        _even_out_kernel,
        grid=(N_TILES,),
        in_specs=[tok(CONV_CH), tok(MLA_HEADS * V_DIM), _full(wo.shape), tok(D_MODEL),
                  _full((1, D_MODEL)), _full(wrh.shape), _full(wrl.shape)],
        out_specs=_front_out_specs(),
        out_shape=_FRONT_OUT_SHAPES,
        scratch_shapes=_front_scratch(),
        compiler_params=_cparams(),
        name="even_out_route",
    )(ya, yb, wo, h, gffn, wrh, wrl)


def _gelu_tanh(x):
    return 0.5 * x * (1.0 + jnp.tanh(math.sqrt(2.0 / math.pi) * (x + 0.044715 * (x * x * x))))


def _odd_kernel(h_ref, g_ref, win_ref, vn_ref, ws_ref, bs_ref, wo_ref, gffn_ref, wrh_ref, wrl_ref,
                h1_ref, route_ref, xs_hbm, dest_out, tab_out, cnt_out, nal_out, mixbuf, *scratch):
    def mixer():
        h = h_ref[...]
        hn = _rms(h, g_ref[...]).astype(BF16)
        z = _gelu_tanh(jnp.dot(hn, win_ref[...], preferred_element_type=F32))
        u = z[:, :SG_WIDTH]
        v = _rms(z[:, SG_WIDTH:], vn_ref[...]).astype(BF16)
        r = lax.broadcasted_iota(I32, (SG_CHUNK, SG_CHUNK), 0)
        c = lax.broadcasted_iota(I32, (SG_CHUNK, SG_CHUNK), 1)
        n_chunks = TS // SG_CHUNK
        dg = SG_WIDTH // SG_GROUPS
        for gi in range(SG_GROUPS):
            gs = slice(gi * dg, (gi + 1) * dg)
            wtril = jnp.where(c <= r, ws_ref[gi], 0.0).astype(BF16)
            rhs = jnp.concatenate([v[ci * SG_CHUNK:(ci + 1) * SG_CHUNK, gs] for ci in range(n_chunks)], axis=1)
            sg = jnp.dot(wtril, rhs, preferred_element_type=F32)
            for ci in range(n_chunks):
                rs = slice(ci * SG_CHUNK, (ci + 1) * SG_CHUNK)
                gate = sg[:, ci * dg:(ci + 1) * dg] + bs_ref[:, gs]
                mixbuf[rs, gs] = (u[rs, gs] * gate).astype(BF16)
        h1 = h + jnp.dot(mixbuf[...], wo_ref[...], preferred_element_type=F32)
        h1_ref[...] = h1
        return h1

    _moe_front(mixer, gffn_ref, wrh_ref, wrl_ref, route_ref, xs_hbm, dest_out, tab_out, cnt_out, nal_out,
               *_split_front_scratch(scratch))


def _odd_layer(h, g, win, vn, ws, bs_full, wo, gffn, wrh, wrl):
    tok = lambda w: pl.BlockSpec((TS, w), lambda i: (i, 0))
    return pl.pallas_call(
        _odd_kernel,
        grid=(N_TILES,),
        in_specs=[tok(D_MODEL), _full((1, D_MODEL)), _full(win.shape), _full((1, SG_WIDTH)),
                  _full(ws.shape), _full(bs_full.shape), _full(wo.shape),
                  _full((1, D_MODEL)), _full(wrh.shape), _full(wrl.shape)],
        out_specs=_front_out_specs(),
        out_shape=_FRONT_OUT_SHAPES,
        scratch_shapes=[pltpu.VMEM((TS, SG_WIDTH), BF16)] + _front_scratch(),
        compiler_params=_cparams(),
        name="odd_mixer_route",
    )(h, g, win, vn, ws, bs_full, wo, gffn, wrh, wrl)


def _gmm_kernel(tab_ref, cnt_ref, nal_ref, wg_ref, wu_ref, wd_ref, xs_hbm, yb_hbm,
                wgb, wub, wdb, xbuf, ybuf, order, first, xsem, ysem, zsem):
    e = pl.program_id(0)
    nblk = nal_ref[0]

    @pl.when(e == 0)
    def _():
        def list_expert(ex, g):
            first[ex] = g
            nb = (cnt_ref[ex] + (BM - 1)) >> BM_SHIFT

            def put(j, c):
                order[g + j] = tab_ref[ex * NJ + j]
                return c

            lax.fori_loop(0, nb, put, 0)
            return g + nb

        first[N_EXPERTS] = lax.fori_loop(0, N_EXPERTS, list_expert, 0)

    def x_copy(g, slot):
        return pltpu.make_async_copy(xs_hbm.at[_chunk_rows(order[g])], xbuf.at[slot], xsem.at[slot])

    def y_copy(g, slot):
        return pltpu.make_async_copy(ybuf.at[slot], yb_hbm.at[_chunk_rows(order[g])], ysem.at[slot])

    @pl.when(e == 0)
    def _():
        for a in range(AHEAD):
            @pl.when(a < nblk)
            def _():
                x_copy(a, a).start()

    g0 = first[e]
    g1 = first[e + 1]

    @pl.when(g1 > g0)
    def _():
        wgb[...] = wg_ref[0].astype(BF16)
        wub[...] = wu_ref[0].astype(BF16)
        wdb[...] = wd_ref[0].astype(BF16)

        def block(g, c):
            slot = g % X_SLOTS
            ys = g & 1

            @pl.when(g + AHEAD < nblk)
            def _():
                x_copy(g + AHEAD, (g + AHEAD) % X_SLOTS).start()

            x_copy(g, slot).wait()

            @pl.when(g >= 2)
            def _():
                y_copy(g - 2, ys).wait()

            x = xbuf[slot].astype(BF16)
            gate = jnp.dot(x, wgb[...], preferred_element_type=F32)
            up = jnp.dot(x, wub[...], preferred_element_type=F32)
            mid = (gate * jax.nn.sigmoid(gate) * up).astype(BF16)
            ybuf[ys] = jnp.dot(mid, wdb[...], preferred_element_type=F32)
            y_copy(g, ys).start()
            return c

        lax.fori_loop(g0, g1, block, 0)

    @pl.when(e == pl.num_programs(0) - 1)
    def _():
        @pl.when(nblk >= 2)
        def _():
            y_copy(nblk - 2, nblk & 1).wait()

        y_copy(nblk - 1, (nblk - 1) & 1).wait()
        xbuf[0] = jnp.zeros((BM, D_MODEL), F32)

        def spare(chunk):
            return pltpu.make_async_copy(xbuf.at[0], yb_hbm.at[_chunk_rows(chunk)], zsem)

        lax.fori_loop(nblk, N_CHUNKS, lambda ch, c: (spare(ch).start(), c)[1], 0)
        lax.fori_loop(nblk, N_CHUNKS, lambda ch, c: (spare(ch).wait(), c)[1], 0)


def _gmm(tab, cnt, nal, xs, w_gate, w_up, w_down, layer):
    wsel = lambda e, *_: (layer * N_EXPERTS + e, 0, 0)
    hbm = pl.BlockSpec(memory_space=pl.ANY)
    return pl.pallas_call(
        _gmm_kernel,
        grid_spec=pltpu.PrefetchScalarGridSpec(
            num_scalar_prefetch=3,
            grid=(N_EXPERTS,),
            in_specs=[pl.BlockSpec((1, D_MODEL, D_EXPERT), wsel),
                      pl.BlockSpec((1, D_MODEL, D_EXPERT), wsel),
                      pl.BlockSpec((1, D_EXPERT, D_MODEL), wsel),
                      hbm],
            out_specs=hbm,
            scratch_shapes=[pltpu.VMEM((D_MODEL, D_EXPERT), BF16), pltpu.VMEM((D_MODEL, D_EXPERT), BF16),
                            pltpu.VMEM((D_EXPERT, D_MODEL), BF16),
                            pltpu.VMEM((X_SLOTS, BM, D_MODEL), F32), pltpu.VMEM((2, BM, D_MODEL), F32),
                            pltpu.SMEM((N_CHUNKS,), I32), pltpu.SMEM((N_EXPERTS + 1,), I32),
                            pltpu.SemaphoreType.DMA((X_SLOTS,)), pltpu.SemaphoreType.DMA((2,)),
                            pltpu.SemaphoreType.DMA],
        ),
        out_shape=jax.ShapeDtypeStruct((N_CHUNKS * BM, D_MODEL), F32),
        compiler_params=_cparams(),
        name="moe_grouped_matmul",
    )(tab, cnt, nal, w_gate, w_up, w_down, xs)


def _combine_ple_kernel(dest_ref, h1_ref, route_ref, p_ref, gple_ref, wgate_ref, wproj_ref, fin_ref,
                        yb_hbm, out_ref, buf0, buf1, buf2, buf3, sem, *, final):
    i = pl.program_id(0)
    n = pl.num_programs(0)
    bufs = (buf0, buf1, buf2, buf3)

    def copies(tile, r, slot):
        t = (tile + SCATTER_LAG) * TS + r
        return [_row_copy(yb_hbm, dest_ref[which * DEST_HALF + t], bufs[slot], which * TS + r, sem.at[slot])
                for which in range(2)]

    def issue_loop(tile, slot):
        def body(r, c):
            for which, cp in enumerate(copies(tile, r, slot)):
                cp.start(priority=which)
            return c

        lax.fori_loop(0, TS, body, 0)

    def issue_unrolled(tile, slot):
        for r in range(TS):
            for which, cp in enumerate(copies(tile, r, slot)):
                cp.start(priority=which)

    def wait_tile(slot):
        def body(r, c):
            _row_copy(yb_hbm, 0, bufs[slot], 0, sem.at[slot]).wait()
            return c

        lax.fori_loop(0, 2 * TS, body, 0, unroll=8)

    def compute(quarter, slot):
        rows = slice(quarter * TS, (quarter + 1) * TS)
        route = route_ref[rows, :]
        w1 = route[:, 2:3]
        w2 = route[:, 3:4]
        h2 = h1_ref[rows, :] + (w1 * bufs[slot][0:TS, :] + w2 * bufs[slot][TS:2 * TS, :])
        gate = jax.nn.sigmoid(jnp.dot(_rms(h2, gple_ref[...]).astype(BF16), wgate_ref[...],
                                      preferred_element_type=F32))
        out = h2 + gate * jnp.dot(p_ref[rows, :].astype(BF16), wproj_ref[...], preferred_element_type=F32)
        if final:
            out = _rms(out, fin_ref[...])
        out_ref[rows, :] = out

    @pl.when(i == 0)
    def _():
        for k in range(AHEAD):
            issue_loop(k, k)

    for k in range(X_SLOTS):
        tile = X_SLOTS * i + k
        ahead = jnp.minimum(tile + AHEAD, N_TILES - 1)
        wait_tile(k)
        issue_unrolled(ahead, (k + AHEAD) % X_SLOTS)
        compute(k, k)

    @pl.when(i == n - 1)
    def _():
        for k in range(AHEAD):
            wait_tile((N_TILES + k) % X_SLOTS)


def _combine_ple(dest, h1, route, p_all, layer, gple, wgate, wproj, fin, yb, final):
    nt = TOKENS // (X_SLOTS * TS)
    tok = lambda w: pl.BlockSpec((X_SLOTS * TS, w), lambda i, *_: (i, 0))
    full = lambda shape: pl.BlockSpec(shape, lambda i, *_: (0,) * len(shape))
    return pl.pallas_call(
        functools.partial(_combine_ple_kernel, final=final),
        grid_spec=pltpu.PrefetchScalarGridSpec(
            num_scalar_prefetch=1,
            grid=(nt,),
            in_specs=[tok(D_MODEL), tok(LANES),
                      pl.BlockSpec((X_SLOTS * TS, PLE_DIM), lambda i, *_: (layer * nt + i, 0)),
                      full((1, D_MODEL)), full(wgate.shape), full(wproj.shape), full((1, D_MODEL)),
                      pl.BlockSpec(memory_space=pl.ANY)],
            out_specs=tok(D_MODEL),
            scratch_shapes=[pltpu.VMEM((2 * TS, D_MODEL), F32)] * X_SLOTS + [pltpu.SemaphoreType.DMA((X_SLOTS,))],
        ),
        out_shape=jax.ShapeDtypeStruct((TOKENS, D_MODEL), F32),
        compiler_params=_cparams(),
        name="moe_combine_ple",
    )(dest, h1, route, p_all, gple, wgate, wproj, fin, yb)


def _router_weights(w_rg, w_re):
    w = jnp.zeros((D_MODEL, LANES), F32).at[:, :N_GROUPS].set(w_rg).at[:, ROUTE_LO:ROUTE_LO + N_EXPERTS].set(w_re)
    hi = w.astype(BF16)
    lo = (w - hi.astype(F32)).astype(BF16)
    return hi, lo


def _even_weights(w_in, w_q_up, w_kv_up):
    base = 3 * CONV_CH + Q_RANK + KV_RANK
    kpe_cols = jnp.zeros((D_MODEL, HEAD_PAD), F32).at[:, ROPE_LO:ROPE_LO + QK_ROPE].set(w_in[:, base:])
    win = jnp.concatenate([w_in[:, :base], kpe_cols], axis=1).astype(BF16)
    dqk = QK_NOPE + QK_ROPE
    wq = jnp.pad(w_q_up.reshape(Q_RANK, MLA_HEADS, dqk), ((0, 0), (0, 0), (0, HEAD_PAD - dqk)))
    wq = wq.reshape(Q_RANK, MLA_HEADS * HEAD_PAD).astype(BF16)
    wkv = w_kv_up.reshape(KV_RANK, MLA_HEADS, QK_NOPE + V_DIM)
    wk = jnp.pad(wkv[:, :, :QK_NOPE], ((0, 0), (0, 0), (0, HEAD_PAD - QK_NOPE))).reshape(KV_RANK, MLA_HEADS * HEAD_PAD)
    wv = wkv[:, :, QK_NOPE:].reshape(KV_RANK, MLA_HEADS * V_DIM)
    return win, wq, jnp.concatenate([wk, wv], axis=1).astype(BF16)


def kernel(x, p, positions, norm_mix, norm_ffn, w_in_e, conv_w, q_norm, w_q_up, kv_norm, w_kv_up, w_out_e,
           w_in_o, v_norm, w_s, b_s, w_out_o, w_router_group, w_router_expert, w_gate, w_up, w_down,
           norm_ple, w_ple_gate, w_ple_proj, final_norm):
    h = x.reshape(TOKENS, D_MODEL)
    p_all = p.reshape(DEPTH * TOKENS, PLE_DIM)
    wg_all = w_gate.reshape(DEPTH * N_EXPERTS, D_MODEL, D_EXPERT)
    wu_all = w_up.reshape(DEPTH * N_EXPERTS, D_MODEL, D_EXPERT)
    wd_all = w_down.reshape(DEPTH * N_EXPERTS, D_EXPERT, D_MODEL)
    ctab, stab = _rope_tables(positions)
    row = lambda a: a.reshape(1, -1)

    for i in range(DEPTH):
        j = i // 2
        wrh, wrl = _router_weights(w_router_group[i], w_router_expert[i])
        if i % 2 == 0:
            win, wq, wkv = _even_weights(w_in_e[j], w_q_up[j], w_kv_up[j])
            ya, q, k, v = _even_front(h, row(norm_mix[i]), win, conv_w[j], row(q_norm[j]), wq,
                                      row(kv_norm[j]), wkv, ctab, stab)
            yb = _attention(q, k, v)
            front = _even_out(ya, yb, w_out_e[j].astype(BF16), h, row(norm_ffn[i]), wrh, wrl)
        else:
            bs_full = jnp.repeat(b_s[j].T, SG_WIDTH // SG_GROUPS, axis=1)
            front = _odd_layer(h, row(norm_mix[i]), w_in_o[j].astype(BF16), row(v_norm[j]), w_s[j], bs_full,
                               w_out_o[j].astype(BF16), row(norm_ffn[i]), wrh, wrl)
        h1, route, xs, dest, tab, cnt, nal = front
        ys = _gmm(tab, cnt, nal, xs, wg_all, wu_all, wd_all, i)
        h = _combine_ple(dest, h1, route, p_all, i, row(norm_ple[i]), w_ple_gate[i].astype(BF16),
                         w_ple_proj[i].astype(BF16), row(final_norm), ys, final=(i == DEPTH - 1))
    return h.reshape(BATCH, SEQ, D_MODEL)
```

```python
import functools
import math

import jax
import jax.numpy as jnp
from jax import lax
from jax.experimental import pallas as pl
from jax.experimental.pallas import tpu as pltpu

F32 = jnp.float32
BF16 = jnp.bfloat16
I32 = jnp.int32

D_MODEL = 1024
BATCH = 4
SEQ = 4096
DEPTH = 4
TOKENS = BATCH * SEQ

CONV_CH = 512
MLA_HEADS = 8
QK_NOPE = 64
QK_ROPE = 32
V_DIM = 64
Q_RANK = 384
KV_RANK = 256
ROPE_THETA = 10000.0
SG_WIDTH = 1024
SG_GROUPS = 8
SG_CHUNK = 128
N_GROUPS = 4
EXP_PER_GROUP = 8
N_EXPERTS = N_GROUPS * EXP_PER_GROUP
D_EXPERT = 512
PLE_DIM = 256
NORM_EPS = 1e-6
NEG_INF = -1e30

LANES = 128
HEAD_PAD = LANES
ROPE_LO = QK_NOPE
ROPE_HALF = QK_ROPE // 2

TS = 256
PARTS = TS // LANES
TQ = 1024
BM = 256
BM_SHIFT = BM.bit_length() - 1
assert 1 << BM_SHIFT == BM and TS == BM
N_TILES = TOKENS // TS
N_CHUNKS = 2 * TOKENS // BM + N_EXPERTS
NJ = TOKENS // BM + 1
XS_ROWS = (N_CHUNKS + 2) * BM
SCATTER_LAG = 2
DEST_HALF = TOKENS + SCATTER_LAG * TS
ROUTE_LO = N_GROUPS
STG_ROWS = 16
CNT_ROW = 4 * PARTS
X_SLOTS = 4
AHEAD = 2

VMEM_LIMIT = 48 * 1024 * 1024

Q_SCALE = math.log2(math.e) / math.sqrt(QK_NOPE + QK_ROPE)


def _cparams(n_axes=1):
    return pltpu.CompilerParams(dimension_semantics=("arbitrary",) * n_axes,
                                vmem_limit_bytes=VMEM_LIMIT)


def _rms(x, g):
    return x * lax.rsqrt(jnp.mean(x * x, axis=-1, keepdims=True) + NORM_EPS) * g


def _full(shape):
    return pl.BlockSpec(shape, lambda *_: (0,) * len(shape))


def _row_copy(src, si, dst, di, sem):
    return pltpu.make_async_copy(src.at[pl.ds(si, 1)], dst.at[pl.ds(di, 1)], sem)


def _chunk_rows(chunk):
    return pl.ds(pl.multiple_of(chunk * BM, BM), BM)


def _rope_table_kernel(pos_ref, freq_ref, c_ref, s_ref):
    ang = pos_ref[...].astype(F32) * freq_ref[...]
    lane = lax.broadcasted_iota(I32, ang.shape, 1)
    cosv = jnp.cos(ang)
    sinv = jnp.sin(ang)
    in_rope = (lane >= ROPE_LO) & (lane < ROPE_LO + QK_ROPE)
    first_half = lane < ROPE_LO + ROPE_HALF
    c_ref[...] = jnp.where(lane < ROPE_LO, 1.0, jnp.where(in_rope, cosv, 0.0))
    s_ref[...] = jnp.where(in_rope, jnp.where(first_half, -sinv, sinv), 0.0)


def _rope_tables(positions):
    inv_freq = 1.0 / (ROPE_THETA ** (jnp.arange(0, QK_ROPE, 2, dtype=F32) / QK_ROPE))
    freq = jnp.zeros((LANES,), F32).at[ROPE_LO:ROPE_LO + QK_ROPE].set(jnp.tile(inv_freq, 2))
    pos = positions.reshape(TOKENS, 1)
    return pl.pallas_call(
        _rope_table_kernel,
        grid=(TOKENS // TS,),
        in_specs=[pl.BlockSpec((TS, 1), lambda i: (i, 0)), _full((1, LANES))],
        out_specs=[pl.BlockSpec((TS, LANES), lambda i: (i, 0))] * 2,
        out_shape=[jax.ShapeDtypeStruct((TOKENS, LANES), F32)] * 2,
        compiler_params=_cparams(),
        name="rope_tables",
    )(pos, freq.reshape(1, LANES))


def _rope(x, c, s, first_half):
    partner = jnp.where(first_half, pltpu.roll(x, LANES - ROPE_HALF, 1), pltpu.roll(x, ROPE_HALF, 1))
    return x * c + partner * s


def _even_front_kernel(h_ref, g_ref, win_ref, cw_ref, qn_ref, wq_ref, kvn_ref, wkv_ref, c_ref, s_ref,
                       ya_ref, q_ref, k_ref, v_ref, zbuf):
    i = pl.program_id(0)
    hn = _rms(h_ref[...], g_ref[...]).astype(BF16)
    proj = jnp.dot(hn, win_ref[...], preferred_element_type=F32)
    xc = proj[:, 0:CONV_CH]
    gb = proj[:, CONV_CH:2 * CONV_CH]
    gc = proj[:, 2 * CONV_CH:3 * CONV_CH]
    o = 3 * CONV_CH
    cq = proj[:, o:o + Q_RANK]
    ckv = proj[:, o + Q_RANK:o + Q_RANK + KV_RANK]
    kpe = proj[:, o + Q_RANK + KV_RANK:]

    @pl.when(i % (SEQ // TS) == 0)
    def _():
        zbuf[0:8, :] = jnp.zeros((8, CONV_CH), F32)

    z = gc * xc
    zbuf[8:8 + TS, :] = z
    z1 = zbuf[7:7 + TS, :]
    z2 = zbuf[6:6 + TS, :]
    cw = cw_ref[...]
    conv = cw[0:1, :] * z2 + cw[1:2, :] * z1 + cw[2:3, :] * z
    ya_ref[...] = (gb * conv).astype(BF16)
    zbuf[0:8, :] = zbuf[TS:TS + 8, :]

    c = c_ref[...]
    s = s_ref[...]
    lane = lax.broadcasted_iota(I32, (TS, LANES), 1)
    first_half = lane < ROPE_LO + ROPE_HALF

    cqn = _rms(cq, qn_ref[...]).astype(BF16)
    q = jnp.dot(cqn, wq_ref[...], preferred_element_type=F32)
    for hd in range(MLA_HEADS):
        sl = slice(hd * HEAD_PAD, (hd + 1) * HEAD_PAD)
        q_ref[:, sl] = (_rope(q[:, sl], c, s, first_half) * Q_SCALE).astype(BF16)

    ckvn = _rms(ckv, kvn_ref[...]).astype(BF16)
    kv = jnp.dot(ckvn, wkv_ref[...], preferred_element_type=F32)
    kper = _rope(kpe, c, s, first_half)
    for hd in range(MLA_HEADS):
        sl = slice(hd * HEAD_PAD, (hd + 1) * HEAD_PAD)
        k_ref[:, sl] = (kv[:, sl] + kper).astype(BF16)
    v_ref[...] = kv[:, MLA_HEADS * HEAD_PAD:].astype(BF16)


def _even_front(h, g, win, cw, qn, wq, kvn, wkv, ctab, stab):
    tok = lambda w: pl.BlockSpec((TS, w), lambda i: (i, 0))
    return pl.pallas_call(
        _even_front_kernel,
        grid=(TOKENS // TS,),
        in_specs=[tok(D_MODEL), _full((1, D_MODEL)), _full(win.shape), _full(cw.shape),
                  _full((1, Q_RANK)), _full(wq.shape), _full((1, KV_RANK)), _full(wkv.shape),
                  tok(LANES), tok(LANES)],
        out_specs=[tok(CONV_CH), tok(MLA_HEADS * HEAD_PAD), tok(MLA_HEADS * HEAD_PAD), tok(MLA_HEADS * V_DIM)],
        out_shape=[jax.ShapeDtypeStruct((TOKENS, CONV_CH), BF16),
                   jax.ShapeDtypeStruct((TOKENS, MLA_HEADS * HEAD_PAD), BF16),
                   jax.ShapeDtypeStruct((TOKENS, MLA_HEADS * HEAD_PAD), BF16),
                   jax.ShapeDtypeStruct((TOKENS, MLA_HEADS * V_DIM), BF16)],
        scratch_shapes=[pltpu.VMEM((TS + 8, CONV_CH), F32)],
        compiler_params=_cparams(),
        name="even_front",
    )(h, g, win, cw, qn, wq, kvn, wkv, ctab, stab)


def _attn_kernel(q_ref, k_ref, v_ref, o_ref):
    i = pl.program_id(2)
    nt = (((1,), (1,)), ((), ()))
    for hh in range(2):
        hs = slice(hh * HEAD_PAD, (hh + 1) * HEAD_PAD)
        q = q_ref[:, hs]

        def step(j, carry):
            m, l, acc = carry
            kb = k_ref[pl.ds(pl.multiple_of(j * TQ, TQ), TQ), hs]
            vb = v_ref[pl.ds(pl.multiple_of(j * TQ, TQ), TQ), :]
            s = lax.dot_general(q, kb, nt, preferred_element_type=F32)
            m_new = jnp.maximum(m, jnp.max(s, axis=-1, keepdims=True))
            alpha = jnp.exp2(m - m_new)
            p = jnp.exp2(s - m_new)
            l = alpha * l + jnp.sum(p, axis=-1, keepdims=True)
            acc = alpha * acc + jnp.dot(p.astype(BF16), vb, preferred_element_type=F32)
            return m_new, l, acc

        init = (jnp.full((TQ, 1), NEG_INF, F32), jnp.zeros((TQ, 1), F32), jnp.zeros((TQ, 2 * V_DIM), F32))
        m, l, acc = lax.fori_loop(0, i, step, init)
        vs = slice(hh * V_DIM, (hh + 1) * V_DIM)
        base = pl.multiple_of(i * TQ, TQ)
        half_rows = TQ // 2
        for half in range(2):
            rs = slice(half * half_rows, (half + 1) * half_rows)
            nk = (half + 1) * half_rows
            kb = k_ref[pl.ds(base, nk), hs]
            vb = v_ref[pl.ds(base, nk), :]
            s = lax.dot_general(q[rs, :], kb, nt, preferred_element_type=F32)
            qpos = lax.broadcasted_iota(I32, (half_rows, nk), 0) + half * half_rows
            kpos = lax.broadcasted_iota(I32, (half_rows, nk), 1)
            s = jnp.where(kpos <= qpos, s, NEG_INF)
            m_new = jnp.maximum(m[rs, :], jnp.max(s, axis=-1, keepdims=True))
            alpha = jnp.exp2(m[rs, :] - m_new)
            p = jnp.exp2(s - m_new)
            l_new = alpha * l[rs, :] + jnp.sum(p, axis=-1, keepdims=True)
            acc_new = alpha * acc[rs, :] + jnp.dot(p.astype(BF16), vb, preferred_element_type=F32)
            o_ref[rs, vs] = (acc_new / l_new)[:, vs].astype(BF16)


def _attention(q, k, v):
    nq = SEQ // TQ
    pairs = MLA_HEADS // 2
    return pl.pallas_call(
        _attn_kernel,
        grid=(BATCH, pairs, nq),
        in_specs=[pl.BlockSpec((TQ, 2 * HEAD_PAD), lambda b, p, i: (b * nq + i, p)),
                  pl.BlockSpec((SEQ, 2 * HEAD_PAD), lambda b, p, i: (b, p)),
                  pl.BlockSpec((SEQ, 2 * V_DIM), lambda b, p, i: (b, p))],
        out_specs=pl.BlockSpec((TQ, 2 * V_DIM), lambda b, p, i: (b * nq + i, p)),
        out_shape=jax.ShapeDtypeStruct((TOKENS, MLA_HEADS * V_DIM), BF16),
        compiler_params=_cparams(3),
        name="mla_attention",
    )(q, k, v)


def _route_vectors(h1, gffn, wr_hi, wr_lo, carry):
    xn = _rms(h1, gffn)
    x_hi = xn.astype(BF16)
    x_lo = (xn - x_hi.astype(F32)).astype(BF16)
    logits = (jnp.dot(x_hi, wr_hi, preferred_element_type=F32)
              + jnp.dot(x_lo, wr_hi, preferred_element_type=F32)
              + jnp.dot(x_hi, wr_lo, preferred_element_type=F32))
    lane = lax.broadcasted_iota(I32, logits.shape, 1).astype(F32)
    big = float(LANES)

    is_g = lane < N_GROUPS
    gl = jnp.where(is_g, logits, NEG_INF)
    gmax = jnp.max(gl, axis=-1, keepdims=True)
    gidx = jnp.min(jnp.where(gl == gmax, lane, big), axis=-1, keepdims=True)
    gsum = jnp.sum(jnp.where(is_g, jnp.exp(gl - gmax), 0.0), axis=-1, keepdims=True)
    g_w = 1.0 / gsum

    lo = ROUTE_LO + gidx * EXP_PER_GROUP
    el = jnp.where((lane >= lo) & (lane < lo + EXP_PER_GROUP), logits, NEG_INF)
    m1 = jnp.max(el, axis=-1, keepdims=True)
    i1 = jnp.min(jnp.where(el == m1, lane, big), axis=-1, keepdims=True)
    el2 = jnp.where(lane == i1, NEG_INF, el)
    m2 = jnp.max(el2, axis=-1, keepdims=True)
    i2 = jnp.min(jnp.where(el2 == m2, lane, big), axis=-1, keepdims=True)
    t = jnp.exp(m2 - m1)
    w1 = g_w / (1.0 + t)
    w2 = g_w * t / (1.0 + t)

    oh1 = lane == i1
    oh2 = lane == i2
    oh = jnp.where(oh1 | oh2, 1.0, 0.0)
    r = lax.broadcasted_iota(I32, (TS, TS), 0)
    cc = lax.broadcasted_iota(I32, (TS, TS), 1)
    tri = jnp.where(cc < r, 1.0, 0.0).astype(BF16)
    before = jnp.dot(tri, oh.astype(BF16), preferred_element_type=F32) + carry[...]
    rank1 = jnp.sum(jnp.where(oh1, before, 0.0), axis=-1, keepdims=True)
    rank2 = jnp.sum(jnp.where(oh2, before, 0.0), axis=-1, keepdims=True)
    carry[...] = carry[...] + jnp.sum(oh, axis=0, keepdims=True)

    hi1 = jnp.floor(rank1 * (1.0 / BM))
    hi2 = jnp.floor(rank2 * (1.0 / BM))
    vals = (i1 - ROUTE_LO, i2 - ROUTE_LO, w1, w2,
            (i1 - ROUTE_LO) * NJ + hi1, (i2 - ROUTE_LO) * NJ + hi2, rank1 - hi1 * BM, rank2 - hi2 * BM)
    route = jnp.zeros_like(logits)
    for n, val in enumerate(vals):
        route = jnp.where(lane == float(n), val, route)
    route_t = route.T
    rows = [route_t[n:n + 1, part * LANES:(part + 1) * LANES] for n in (4, 5, 6, 7) for part in range(PARTS)]
    rows.append(carry[...])
    rows.append(jnp.zeros((STG_ROWS - CNT_ROW - 1, LANES), F32))
    return xn, route, jnp.concatenate(rows, axis=0).astype(I32)


def _moe_front(mixer, gffn_ref, wrh_ref, wrl_ref, route_ref, xs_hbm, dest_out, tab_out, cnt_out, nal_out,
               carry, xns, stg_v, stg_s, tab, cntp, nal, ssem, gsem, zsem):
    i = pl.program_id(0)
    n = pl.num_programs(0)

    def stage_copy(slot):
        return pltpu.make_async_copy(stg_v, stg_s.at[slot], ssem.at[slot])

    def hand_out_chunks(slot):
        def body(e, na):
            c_new = stg_s[slot, CNT_ROW, ROUTE_LO + e]
            cb = (cntp[e] + (BM - 1)) >> BM_SHIFT
            ca = (c_new + (BM - 1)) >> BM_SHIFT
            tab[e * NJ + cb] = na
            cntp[e] = c_new
            return na + (ca - cb)

        nal[0] = lax.fori_loop(0, N_EXPERTS, body, nal[0])

    def dest_row(slot, which, part, lane):
        chunk = tab[stg_s[slot, which * PARTS + part, lane]]
        return (chunk << BM_SHIFT) + stg_s[slot, (2 + which) * PARTS + part, lane]

    def scatter_row(slot, tile, which, part, lane):
        r = part * LANES + lane
        d = dest_row(slot, which, part, lane)
        dest_out[which * DEST_HALF + (tile + SCATTER_LAG) * TS + r] = d
        _row_copy(xns[slot], r, xs_hbm, d, gsem.at[slot]).start(priority=which)

    def wait_scatter(slot):
        def body(r, c):
            _row_copy(xns[slot], 0, xs_hbm, 0, gsem.at[slot]).wait()
            return c

        lax.fori_loop(0, 2 * TS, body, 0, unroll=8)

    @pl.when(i == 0)
    def _():
        carry[...] = jnp.zeros_like(carry)
        for slot in range(X_SLOTS - SCATTER_LAG, X_SLOTS):
            xns[slot][...] = jnp.zeros((TS, D_MODEL), F32)
            for which in range(2):
                for part in range(PARTS):
                    def placeholder(l, c, slot=slot, which=which, part=part):
                        stg_s[slot, which * PARTS + part, l] = N_EXPERTS * NJ + which
                        stg_s[slot, (2 + which) * PARTS + part, l] = part * LANES + l
                        return c

                    lax.fori_loop(0, LANES, placeholder, 0)

        def clear_tab(j, c):
            tab[j] = 0
            return c

        lax.fori_loop(0, N_EXPERTS * NJ, clear_tab, 0)
        tab[N_EXPERTS * NJ] = N_CHUNKS
        tab[N_EXPERTS * NJ + 1] = N_CHUNKS + 1

        def clear_cnt(e, c):
            cntp[e] = 0
            return c

        lax.fori_loop(0, N_EXPERTS, clear_cnt, 0)
        nal[0] = 0

    def scatter_loop(slot, tile):
        for part in range(PARTS):
            def scatter_lane(lane, c, part=part):
                for which in range(2):
                    scatter_row(slot, tile, which, part, lane)
                return c

            lax.fori_loop(0, LANES, scatter_lane, 0, unroll=8)

    def variant(k):
        prev = (k + X_SLOTS - 1) % X_SLOTS
        lagged = (k + X_SLOTS - SCATTER_LAG) % X_SLOTS

        def run():
            @pl.when(i >= X_SLOTS)
            def _():
                wait_scatter(k)

            @pl.when(i >= 0)
            def _():
                for part in range(PARTS):
                    for lane in range(LANES):
                        for which in range(2):
                            scatter_row(lagged, i - SCATTER_LAG, which, part, lane)

            xn, route, stage = _route_vectors(mixer(), gffn_ref[...], wrh_ref[...], wrl_ref[...], carry)
            xns[k][...] = xn
            route_ref[...] = route

            @pl.when(i >= 1)
            def _():
                stage_copy(prev).wait()
                hand_out_chunks(prev)

            stg_v[...] = stage
            stage_copy(k).start()

        return run

    lax.switch(i % X_SLOTS, [variant(k) for k in range(X_SLOTS)])

    for step in range(SCATTER_LAG):
        @pl.when(i == step)
        def _():
            wait_scatter(X_SLOTS - SCATTER_LAG + step)

    @pl.when(i == n - 1)
    def _():
        for k in range(X_SLOTS):
            @pl.when(i % X_SLOTS == k)
            def _():
                scatter_loop((k + X_SLOTS - 1) % X_SLOTS, i - 1)
                stage_copy(k).wait()
                hand_out_chunks(k)
                scatter_loop(k, i)

        for k in range(X_SLOTS):
            wait_scatter(k)

        xns[0][...] = jnp.zeros((TS, D_MODEL), F32)

        def pad_expert(e, c):
            used = cntp[e] & (BM - 1)
            base = tab[e * NJ + (cntp[e] >> BM_SHIFT)] << BM_SHIFT
            stop = jnp.where(used > 0, BM, 0)
            lax.fori_loop(used, stop, lambda r, c2: (_row_copy(xns[0], 0, xs_hbm, base + r, zsem).start(), c2)[1], 0)
            lax.fori_loop(used, stop, lambda r, c2: (_row_copy(xns[0], 0, xs_hbm, base + r, zsem).wait(), c2)[1], 0)
            cnt_out[e] = cntp[e]
            return c

        lax.fori_loop(0, N_EXPERTS, pad_expert, 0)

        def spare(chunk):
            return pltpu.make_async_copy(xns[0], xs_hbm.at[_chunk_rows(chunk)], zsem)

        lax.fori_loop(nal[0], N_CHUNKS, lambda ch, c: (spare(ch).start(), c)[1], 0)
        lax.fori_loop(nal[0], N_CHUNKS, lambda ch, c: (spare(ch).wait(), c)[1], 0)

        def copy_tab(j, c):
            tab_out[j] = tab[j]
            return c

        lax.fori_loop(0, N_EXPERTS * NJ, copy_tab, 0)
        nal_out[0] = nal[0]


_FRONT_OUT_SHAPES = [jax.ShapeDtypeStruct((TOKENS, D_MODEL), F32),
                     jax.ShapeDtypeStruct((TOKENS, LANES), F32),
                     jax.ShapeDtypeStruct((XS_ROWS, D_MODEL), F32),
                     jax.ShapeDtypeStruct((2 * DEST_HALF,), I32),
                     jax.ShapeDtypeStruct((N_EXPERTS * NJ,), I32),
                     jax.ShapeDtypeStruct((N_EXPERTS,), I32),
                     jax.ShapeDtypeStruct((1,), I32)]


def _front_out_specs():
    tok = lambda w: pl.BlockSpec((TS, w), lambda i: (i, 0))
    smem = lambda: pl.BlockSpec(memory_space=pltpu.SMEM)
    return [tok(D_MODEL), tok(LANES), pl.BlockSpec(memory_space=pl.ANY), smem(), smem(), smem(), smem()]


def _front_scratch():
    return ([pltpu.VMEM((1, LANES), F32)] + [pltpu.VMEM((TS, D_MODEL), F32)] * X_SLOTS
            + [pltpu.VMEM((STG_ROWS, LANES), I32), pltpu.SMEM((X_SLOTS, STG_ROWS, LANES), I32),
               pltpu.SMEM((N_EXPERTS * NJ + 2,), I32), pltpu.SMEM((N_EXPERTS,), I32), pltpu.SMEM((1,), I32),
               pltpu.SemaphoreType.DMA((X_SLOTS,)), pltpu.SemaphoreType.DMA((X_SLOTS,)), pltpu.SemaphoreType.DMA])


def _split_front_scratch(scratch):
    carry = scratch[0]
    xns = tuple(scratch[1:1 + X_SLOTS])
    return (carry, xns) + tuple(scratch[1 + X_SLOTS:])


def _even_out_kernel(ya_ref, yb_ref, wo_ref, h_ref, gffn_ref, wrh_ref, wrl_ref,
                     h1_ref, route_ref, xs_hbm, dest_out, tab_out, cnt_out, nal_out, *scratch):
    def mixer():
        mix = (jnp.dot(ya_ref[...], wo_ref[0:CONV_CH, :], preferred_element_type=F32)
               + jnp.dot(yb_ref[...], wo_ref[CONV_CH:, :], preferred_element_type=F32))
        h1 = h_ref[...] + mix
        h1_ref[...] = h1
        return h1

    _moe_front(mixer, gffn_ref, wrh_ref, wrl_ref, route_ref, xs_hbm, dest_out, tab_out, cnt_out, nal_out,
               *_split_front_scratch(scratch))


def _even_out(ya, yb, wo, h, gffn, wrh, wrl):
    tok = lambda w: pl.BlockSpec((TS, w), lambda i: (i, 0))
    return pl.pallas_call(
        _even_out_kernel,
        grid=(N_TILES,),
        in_specs=[tok(CONV_CH), tok(MLA_HEADS * V_DIM), _full(wo.shape), tok(D_MODEL),
                  _full((1, D_MODEL)), _full(wrh.shape), _full(wrl.shape)],
        out_specs=_front_out_specs(),
        out_shape=_FRONT_OUT_SHAPES,
        scratch_shapes=_front_scratch(),
        compiler_params=_cparams(),
        name="even_out_route",
    )(ya, yb, wo, h, gffn, wrh, wrl)


def _gelu_tanh(x):
    return 0.5 * x * (1.0 + jnp.tanh(math.sqrt(2.0 / math.pi) * (x + 0.044715 * (x * x * x))))


def _odd_kernel(h_ref, g_ref, win_ref, vn_ref, ws_ref, bs_ref, wo_ref, gffn_ref, wrh_ref, wrl_ref,
                h1_ref, route_ref, xs_hbm, dest_out, tab_out, cnt_out, nal_out, mixbuf, *scratch):
    def mixer():
        h = h_ref[...]
        hn = _rms(h, g_ref[...]).astype(BF16)
        z = _gelu_tanh(jnp.dot(hn, win_ref[...], preferred_element_type=F32))
        u = z[:, :SG_WIDTH]
        v = _rms(z[:, SG_WIDTH:], vn_ref[...]).astype(BF16)
        r = lax.broadcasted_iota(I32, (SG_CHUNK, SG_CHUNK), 0)
        c = lax.broadcasted_iota(I32, (SG_CHUNK, SG_CHUNK), 1)
        n_chunks = TS // SG_CHUNK
        dg = SG_WIDTH // SG_GROUPS
        for gi in range(SG_GROUPS):
            gs = slice(gi * dg, (gi + 1) * dg)
            wtril = jnp.where(c <= r, ws_ref[gi], 0.0).astype(BF16)
            rhs = jnp.concatenate([v[ci * SG_CHUNK:(ci + 1) * SG_CHUNK, gs] for ci in range(n_chunks)], axis=1)
            sg = jnp.dot(wtril, rhs, preferred_element_type=F32)
            for ci in range(n_chunks):
                rs = slice(ci * SG_CHUNK, (ci + 1) * SG_CHUNK)
                gate = sg[:, ci * dg:(ci + 1) * dg] + bs_ref[:, gs]
                mixbuf[rs, gs] = (u[rs, gs] * gate).astype(BF16)
        h1 = h + jnp.dot(mixbuf[...], wo_ref[...], preferred_element_type=F32)
        h1_ref[...] = h1
        return h1

    _moe_front(mixer, gffn_ref, wrh_ref, wrl_ref, route_ref, xs_hbm, dest_out, tab_out, cnt_out, nal_out,
               *_split_front_scratch(scratch))


def _odd_layer(h, g, win, vn, ws, bs_full, wo, gffn, wrh, wrl):
    tok = lambda w: pl.BlockSpec((TS, w), lambda i: (i, 0))
    return pl.pallas_call(
        _odd_kernel,
        grid=(N_TILES,),
        in_specs=[tok(D_MODEL), _full((1, D_MODEL)), _full(win.shape), _full((1, SG_WIDTH)),
                  _full(ws.shape), _full(bs_full.shape), _full(wo.shape),
                  _full((1, D_MODEL)), _full(wrh.shape), _full(wrl.shape)],
        out_specs=_front_out_specs(),
        out_shape=_FRONT_OUT_SHAPES,
        scratch_shapes=[pltpu.VMEM((TS, SG_WIDTH), BF16)] + _front_scratch(),
        compiler_params=_cparams(),
        name="odd_mixer_route",
    )(h, g, win, vn, ws, bs_full, wo, gffn, wrh, wrl)


def _gmm_kernel(tab_ref, cnt_ref, nal_ref, wg_ref, wu_ref, wd_ref, xs_hbm, yb_hbm,
                wgb, wub, wdb, xbuf, ybuf, order, first, xsem, ysem, zsem):
    e = pl.program_id(0)
    nblk = nal_ref[0]

    @pl.when(e == 0)
    def _():
        def list_expert(ex, g):
            first[ex] = g
            nb = (cnt_ref[ex] + (BM - 1)) >> BM_SHIFT

            def put(j, c):
                order[g + j] = tab_ref[ex * NJ + j]
                return c

            lax.fori_loop(0, nb, put, 0)
            return g + nb

        first[N_EXPERTS] = lax.fori_loop(0, N_EXPERTS, list_expert, 0)

    def x_copy(g, slot):
        return pltpu.make_async_copy(xs_hbm.at[_chunk_rows(order[g])], xbuf.at[slot], xsem.at[slot])

    def y_copy(g, slot):
        return pltpu.make_async_copy(ybuf.at[slot], yb_hbm.at[_chunk_rows(order[g])], ysem.at[slot])

    @pl.when(e == 0)
    def _():
        for a in range(AHEAD):
            @pl.when(a < nblk)
            def _():
                x_copy(a, a).start()

    g0 = first[e]
    g1 = first[e + 1]

    @pl.when(g1 > g0)
    def _():
        wgb[...] = wg_ref[0].astype(BF16)
        wub[...] = wu_ref[0].astype(BF16)
        wdb[...] = wd_ref[0].astype(BF16)

        def block(g, c):
            slot = g % X_SLOTS
            ys = g & 1

            @pl.when(g + AHEAD < nblk)
            def _():
                x_copy(g + AHEAD, (g + AHEAD) % X_SLOTS).start()

            x_copy(g, slot).wait()

            @pl.when(g >= 2)
            def _():
                y_copy(g - 2, ys).wait()

            x = xbuf[slot].astype(BF16)
            gate = jnp.dot(x, wgb[...], preferred_element_type=F32)
            up = jnp.dot(x, wub[...], preferred_element_type=F32)
            mid = (gate * jax.nn.sigmoid(gate) * up).astype(BF16)
            ybuf[ys] = jnp.dot(mid, wdb[...], preferred_element_type=F32)
            y_copy(g, ys).start()
            return c

        lax.fori_loop(g0, g1, block, 0)

    @pl.when(e == pl.num_programs(0) - 1)
    def _():
        @pl.when(nblk >= 2)
        def _():
            y_copy(nblk - 2, nblk & 1).wait()

        y_copy(nblk - 1, (nblk - 1) & 1).wait()
        xbuf[0] = jnp.zeros((BM, D_MODEL), F32)

        def spare(chunk):
            return pltpu.make_async_copy(xbuf.at[0], yb_hbm.at[_chunk_rows(chunk)], zsem)

        lax.fori_loop(nblk, N_CHUNKS, lambda ch, c: (spare(ch).start(), c)[1], 0)
        lax.fori_loop(nblk, N_CHUNKS, lambda ch, c: (spare(ch).wait(), c)[1], 0)


def _gmm(tab, cnt, nal, xs, w_gate, w_up, w_down, layer):
    wsel = lambda e, *_: (layer * N_EXPERTS + e, 0, 0)
    hbm = pl.BlockSpec(memory_space=pl.ANY)
    return pl.pallas_call(
        _gmm_kernel,
        grid_spec=pltpu.PrefetchScalarGridSpec(
            num_scalar_prefetch=3,
            grid=(N_EXPERTS,),
            in_specs=[pl.BlockSpec((1, D_MODEL, D_EXPERT), wsel),
                      pl.BlockSpec((1, D_MODEL, D_EXPERT), wsel),
                      pl.BlockSpec((1, D_EXPERT, D_MODEL), wsel),
                      hbm],
            out_specs=hbm,
            scratch_shapes=[pltpu.VMEM((D_MODEL, D_EXPERT), BF16), pltpu.VMEM((D_MODEL, D_EXPERT), BF16),
                            pltpu.VMEM((D_EXPERT, D_MODEL), BF16),
                            pltpu.VMEM((X_SLOTS, BM, D_MODEL), F32), pltpu.VMEM((2, BM, D_MODEL), F32),
                            pltpu.SMEM((N_CHUNKS,), I32), pltpu.SMEM((N_EXPERTS + 1,), I32),
                            pltpu.SemaphoreType.DMA((X_SLOTS,)), pltpu.SemaphoreType.DMA((2,)),
                            pltpu.SemaphoreType.DMA],
        ),
        out_shape=jax.ShapeDtypeStruct((N_CHUNKS * BM, D_MODEL), F32),
        compiler_params=_cparams(),
        name="moe_grouped_matmul",
    )(tab, cnt, nal, w_gate, w_up, w_down, xs)


def _combine_ple_kernel(dest_ref, h1_ref, route_ref, p_ref, gple_ref, wgate_ref, wproj_ref, fin_ref,
                        yb_hbm, out_ref, buf0, buf1, buf2, buf3, sem, *, final):
    i = pl.program_id(0)
    n = pl.num_programs(0)
    bufs = (buf0, buf1, buf2, buf3)

    def copies(tile, r, slot):
        t = (tile + SCATTER_LAG) * TS + r
        return [_row_copy(yb_hbm, dest_ref[which * DEST_HALF + t], bufs[slot], which * TS + r, sem.at[slot])
                for which in range(2)]

    def issue_loop(tile, slot):
        def body(r, c):
            for which, cp in enumerate(copies(tile, r, slot)):
                cp.start(priority=which)
            return c

        lax.fori_loop(0, TS, body, 0)

    def issue_unrolled(tile, slot):
        for r in range(TS):
            for which, cp in enumerate(copies(tile, r, slot)):
                cp.start(priority=which)

    def wait_tile(slot):
        def body(r, c):
            _row_copy(yb_hbm, 0, bufs[slot], 0, sem.at[slot]).wait()
            return c

        lax.fori_loop(0, 2 * TS, body, 0, unroll=8)

    def compute(quarter, slot):
        rows = slice(quarter * TS, (quarter + 1) * TS)
        route = route_ref[rows, :]
        w1 = route[:, 2:3]
        w2 = route[:, 3:4]
        h2 = h1_ref[rows, :] + (w1 * bufs[slot][0:TS, :] + w2 * bufs[slot][TS:2 * TS, :])
        gate = jax.nn.sigmoid(jnp.dot(_rms(h2, gple_ref[...]).astype(BF16), wgate_ref[...],
                                      preferred_element_type=F32))
        out = h2 + gate * jnp.dot(p_ref[rows, :].astype(BF16), wproj_ref[...], preferred_element_type=F32)
        if final:
            out = _rms(out, fin_ref[...])
        out_ref[rows, :] = out

    @pl.when(i == 0)
    def _():
        for k in range(AHEAD):
            issue_loop(k, k)

    for k in range(X_SLOTS):
        tile = X_SLOTS * i + k
        ahead = jnp.minimum(tile + AHEAD, N_TILES - 1)
        wait_tile(k)
        issue_unrolled(ahead, (k + AHEAD) % X_SLOTS)
        compute(k, k)

    @pl.when(i == n - 1)
    def _():
        for k in range(AHEAD):
            wait_tile((N_TILES + k) % X_SLOTS)


def _combine_ple(dest, h1, route, p_all, layer, gple, wgate, wproj, fin, yb, final):
    nt = TOKENS // (X_SLOTS * TS)
    tok = lambda w: pl.BlockSpec((X_SLOTS * TS, w), lambda i, *_: (i, 0))
    full = lambda shape: pl.BlockSpec(shape, lambda i, *_: (0,) * len(shape))
    return pl.pallas_call(
        functools.partial(_combine_ple_kernel, final=final),
        grid_spec=pltpu.PrefetchScalarGridSpec(
            num_scalar_prefetch=1,
            grid=(nt,),
            in_specs=[tok(D_MODEL), tok(LANES),
                      pl.BlockSpec((X_SLOTS * TS, PLE_DIM), lambda i, *_: (layer * nt + i, 0)),
                      full((1, D_MODEL)), full(wgate.shape), full(wproj.shape), full((1, D_MODEL)),
                      pl.BlockSpec(memory_space=pl.ANY)],
            out_specs=tok(D_MODEL),
            scratch_shapes=[pltpu.VMEM((2 * TS, D_MODEL), F32)] * X_SLOTS + [pltpu.SemaphoreType.DMA((X_SLOTS,))],
        ),
        out_shape=jax.ShapeDtypeStruct((TOKENS, D_MODEL), F32),
        compiler_params=_cparams(),
        name="moe_combine_ple",
    )(dest, h1, route, p_all, gple, wgate, wproj, fin, yb)


def _router_weights(w_rg, w_re):
    w = jnp.zeros((D_MODEL, LANES), F32).at[:, :N_GROUPS].set(w_rg).at[:, ROUTE_LO:ROUTE_LO + N_EXPERTS].set(w_re)
    hi = w.astype(BF16)
    lo = (w - hi.astype(F32)).astype(BF16)
    return hi, lo


def _even_weights(w_in, w_q_up, w_kv_up):
    base = 3 * CONV_CH + Q_RANK + KV_RANK
    kpe_cols = jnp.zeros((D_MODEL, HEAD_PAD), F32).at[:, ROPE_LO:ROPE_LO + QK_ROPE].set(w_in[:, base:])
    win = jnp.concatenate([w_in[:, :base], kpe_cols], axis=1).astype(BF16)
    dqk = QK_NOPE + QK_ROPE
    wq = jnp.pad(w_q_up.reshape(Q_RANK, MLA_HEADS, dqk), ((0, 0), (0, 0), (0, HEAD_PAD - dqk)))
    wq = wq.reshape(Q_RANK, MLA_HEADS * HEAD_PAD).astype(BF16)
    wkv = w_kv_up.reshape(KV_RANK, MLA_HEADS, QK_NOPE + V_DIM)
    wk = jnp.pad(wkv[:, :, :QK_NOPE], ((0, 0), (0, 0), (0, HEAD_PAD - QK_NOPE))).reshape(KV_RANK, MLA_HEADS * HEAD_PAD)
    wv = wkv[:, :, QK_NOPE:].reshape(KV_RANK, MLA_HEADS * V_DIM)
    return win, wq, jnp.concatenate([wk, wv], axis=1).astype(BF16)


def kernel(x, p, positions, norm_mix, norm_ffn, w_in_e, conv_w, q_norm, w_q_up, kv_norm, w_kv_up, w_out_e,
           w_in_o, v_norm, w_s, b_s, w_out_o, w_router_group, w_router_expert, w_gate, w_up, w_down,
           norm_ple, w_ple_gate, w_ple_proj, final_norm):
    h = x.reshape(TOKENS, D_MODEL)
    p_all = p.reshape(DEPTH * TOKENS, PLE_DIM)
    wg_all = w_gate.reshape(DEPTH * N_EXPERTS, D_MODEL, D_EXPERT)
    wu_all = w_up.reshape(DEPTH * N_EXPERTS, D_MODEL, D_EXPERT)
    wd_all = w_down.reshape(DEPTH * N_EXPERTS, D_EXPERT, D_MODEL)
    ctab, stab = _rope_tables(positions)
    row = lambda a: a.reshape(1, -1)

    for i in range(DEPTH):
        j = i // 2
        wrh, wrl = _router_weights(w_router_group[i], w_router_expert[i])
        if i % 2 == 0:
            win, wq, wkv = _even_weights(w_in_e[j], w_q_up[j], w_kv_up[j])
            ya, q, k, v = _even_front(h, row(norm_mix[i]), win, conv_w[j], row(q_norm[j]), wq,
                                      row(kv_norm[j]), wkv, ctab, stab)
            yb = _attention(q, k, v)
            front = _even_out(ya, yb, w_out_e[j].astype(BF16), h, row(norm_ffn[i]), wrh, wrl)
        else:
            bs_full = jnp.repeat(b_s[j].T, SG_WIDTH // SG_GROUPS, axis=1)
            front = _odd_layer(h, row(norm_mix[i]), w_in_o[j].astype(BF16), row(v_norm[j]), w_s[j], bs_full,
                               w_out_o[j].astype(BF16), row(norm_ffn[i]), wrh, wrl)
        h1, route, xs, dest, tab, cnt, nal = front
        ys = _gmm(tab, cnt, nal, xs, wg_all, wu_all, wd_all, i)
        h = _combine_ple(dest, h1, route, p_all, i, row(norm_ple[i]), w_ple_gate[i].astype(BF16),
                         w_ple_proj[i].astype(BF16), row(final_norm), ys, final=(i == DEPTH - 1))
    return h.reshape(BATCH, SEQ, D_MODEL)
```

```python
import functools
import math

import jax
import jax.numpy as jnp
from jax import lax
from jax.experimental import pallas as pl
from jax.experimental.pallas import tpu as pltpu

F32 = jnp.float32
BF16 = jnp.bfloat16
I32 = jnp.int32

D_MODEL = 1024
BATCH = 4
SEQ = 4096
DEPTH = 4
TOKENS = BATCH * SEQ

CONV_CH = 512
MLA_HEADS = 8
QK_NOPE = 64
QK_ROPE = 32
V_DIM = 64
Q_RANK = 384
KV_RANK = 256
ROPE_THETA = 10000.0
SG_WIDTH = 1024
SG_GROUPS = 8
SG_CHUNK = 128
N_GROUPS = 4
EXP_PER_GROUP = 8
N_EXPERTS = N_GROUPS * EXP_PER_GROUP
D_EXPERT = 512
PLE_DIM = 256
NORM_EPS = 1e-6
NEG_INF = -1e30

LANES = 128
HEAD_PAD = LANES
ROPE_LO = QK_NOPE
ROPE_HALF = QK_ROPE // 2

TS = 256
PARTS = TS // LANES
TQ = 1024
BM = 256
BM_SHIFT = BM.bit_length() - 1
assert 1 << BM_SHIFT == BM and TS == BM
N_TILES = TOKENS // TS
N_CHUNKS = 2 * TOKENS // BM + N_EXPERTS
NJ = TOKENS // BM + 1
XS_ROWS = (N_CHUNKS + 2) * BM
SCATTER_LAG = 2
DEST_HALF = TOKENS + SCATTER_LAG * TS
ROUTE_LO = N_GROUPS
STG_ROWS = 16
CNT_ROW = 4 * PARTS
X_SLOTS = 4
AHEAD = 3

VMEM_LIMIT = 48 * 1024 * 1024

Q_SCALE = math.log2(math.e) / math.sqrt(QK_NOPE + QK_ROPE)


def _cparams(n_axes=1):
    return pltpu.CompilerParams(dimension_semantics=("arbitrary",) * n_axes,
                                vmem_limit_bytes=VMEM_LIMIT)


def _rms(x, g):
    return x * lax.rsqrt(jnp.mean(x * x, axis=-1, keepdims=True) + NORM_EPS) * g


def _full(shape):
    return pl.BlockSpec(shape, lambda *_: (0,) * len(shape))


def _row_copy(src, si, dst, di, sem):
    return pltpu.make_async_copy(src.at[pl.ds(si, 1)], dst.at[pl.ds(di, 1)], sem)


def _chunk_rows(chunk):
    return pl.ds(pl.multiple_of(chunk * BM, BM), BM)


def _rope_table_kernel(pos_ref, freq_ref, c_ref, s_ref):
    ang = pos_ref[...].astype(F32) * freq_ref[...]
    lane = lax.broadcasted_iota(I32, ang.shape, 1)
    cosv = jnp.cos(ang)
    sinv = jnp.sin(ang)
    in_rope = (lane >= ROPE_LO) & (lane < ROPE_LO + QK_ROPE)
    first_half = lane < ROPE_LO + ROPE_HALF
    c_ref[...] = jnp.where(lane < ROPE_LO, 1.0, jnp.where(in_rope, cosv, 0.0))
    s_ref[...] = jnp.where(in_rope, jnp.where(first_half, -sinv, sinv), 0.0)


def _rope_tables(positions):
    inv_freq = 1.0 / (ROPE_THETA ** (jnp.arange(0, QK_ROPE, 2, dtype=F32) / QK_ROPE))
    freq = jnp.zeros((LANES,), F32).at[ROPE_LO:ROPE_LO + QK_ROPE].set(jnp.tile(inv_freq, 2))
    pos = positions.reshape(TOKENS, 1)
    return pl.pallas_call(
        _rope_table_kernel,
        grid=(TOKENS // TS,),
        in_specs=[pl.BlockSpec((TS, 1), lambda i: (i, 0)), _full((1, LANES))],
        out_specs=[pl.BlockSpec((TS, LANES), lambda i: (i, 0))] * 2,
        out_shape=[jax.ShapeDtypeStruct((TOKENS, LANES), F32)] * 2,
        compiler_params=_cparams(),
        name="rope_tables",
    )(pos, freq.reshape(1, LANES))


def _rope(x, c, s, first_half):
    partner = jnp.where(first_half, pltpu.roll(x, LANES - ROPE_HALF, 1), pltpu.roll(x, ROPE_HALF, 1))
    return x * c + partner * s


def _even_front_kernel(h_ref, g_ref, win_ref, cw_ref, qn_ref, wq_ref, kvn_ref, wkv_ref, c_ref, s_ref,
                       ya_ref, q_ref, k_ref, v_ref, zbuf):
    i = pl.program_id(0)
    hn = _rms(h_ref[...], g_ref[...]).astype(BF16)
    proj = jnp.dot(hn, win_ref[...], preferred_element_type=F32)
    xc = proj[:, 0:CONV_CH]
    gb = proj[:, CONV_CH:2 * CONV_CH]
    gc = proj[:, 2 * CONV_CH:3 * CONV_CH]
    o = 3 * CONV_CH
    cq = proj[:, o:o + Q_RANK]
    ckv = proj[:, o + Q_RANK:o + Q_RANK + KV_RANK]
    kpe = proj[:, o + Q_RANK + KV_RANK:]

    @pl.when(i % (SEQ // TS) == 0)
    def _():
        zbuf[0:8, :] = jnp.zeros((8, CONV_CH), F32)

    z = gc * xc
    zbuf[8:8 + TS, :] = z
    z1 = zbuf[7:7 + TS, :]
    z2 = zbuf[6:6 + TS, :]
    cw = cw_ref[...]
    conv = cw[0:1, :] * z2 + cw[1:2, :] * z1 + cw[2:3, :] * z
    ya_ref[...] = (gb * conv).astype(BF16)
    zbuf[0:8, :] = zbuf[TS:TS + 8, :]

    c = c_ref[...]
    s = s_ref[...]
    lane = lax.broadcasted_iota(I32, (TS, LANES), 1)
    first_half = lane < ROPE_LO + ROPE_HALF

    cqn = _rms(cq, qn_ref[...]).astype(BF16)
    q = jnp.dot(cqn, wq_ref[...], preferred_element_type=F32)
    for hd in range(MLA_HEADS):
        sl = slice(hd * HEAD_PAD, (hd + 1) * HEAD_PAD)
        q_ref[:, sl] = (_rope(q[:, sl], c, s, first_half) * Q_SCALE).astype(BF16)

    ckvn = _rms(ckv, kvn_ref[...]).astype(BF16)
    kv = jnp.dot(ckvn, wkv_ref[...], preferred_element_type=F32)
    kper = _rope(kpe, c, s, first_half)
    for hd in range(MLA_HEADS):
        sl = slice(hd * HEAD_PAD, (hd + 1) * HEAD_PAD)
        k_ref[:, sl] = (kv[:, sl] + kper).astype(BF16)
    v_ref[...] = kv[:, MLA_HEADS * HEAD_PAD:].astype(BF16)


def _even_front(h, g, win, cw, qn, wq, kvn, wkv, ctab, stab):
    tok = lambda w: pl.BlockSpec((TS, w), lambda i: (i, 0))
    return pl.pallas_call(
        _even_front_kernel,
        grid=(TOKENS // TS,),
        in_specs=[tok(D_MODEL), _full((1, D_MODEL)), _full(win.shape), _full(cw.shape),
                  _full((1, Q_RANK)), _full(wq.shape), _full((1, KV_RANK)), _full(wkv.shape),
                  tok(LANES), tok(LANES)],
        out_specs=[tok(CONV_CH), tok(MLA_HEADS * HEAD_PAD), tok(MLA_HEADS * HEAD_PAD), tok(MLA_HEADS * V_DIM)],
        out_shape=[jax.ShapeDtypeStruct((TOKENS, CONV_CH), BF16),
                   jax.ShapeDtypeStruct((TOKENS, MLA_HEADS * HEAD_PAD), BF16),
                   jax.ShapeDtypeStruct((TOKENS, MLA_HEADS * HEAD_PAD), BF16),
                   jax.ShapeDtypeStruct((TOKENS, MLA_HEADS * V_DIM), BF16)],
        scratch_shapes=[pltpu.VMEM((TS + 8, CONV_CH), F32)],
        compiler_params=_cparams(),
        name="even_front",
    )(h, g, win, cw, qn, wq, kvn, wkv, ctab, stab)


def _attn_kernel(q_ref, k_ref, v_ref, o_ref):
    i = pl.program_id(2)
    nt = (((1,), (1,)), ((), ()))
    for hh in range(2):
        hs = slice(hh * HEAD_PAD, (hh + 1) * HEAD_PAD)
        q = q_ref[:, hs]

        def step(j, carry):
            m, l, acc = carry
            kb = k_ref[pl.ds(pl.multiple_of(j * TQ, TQ), TQ), hs]
            vb = v_ref[pl.ds(pl.multiple_of(j * TQ, TQ), TQ), :]
            s = lax.dot_general(q, kb, nt, preferred_element_type=F32)
            m_new = jnp.maximum(m, jnp.max(s, axis=-1, keepdims=True))
            alpha = jnp.exp2(m - m_new)
            p = jnp.exp2(s - m_new)
            l = alpha * l + jnp.sum(p, axis=-1, keepdims=True)
            acc = alpha * acc + jnp.dot(p.astype(BF16), vb, preferred_element_type=F32)
            return m_new, l, acc

        init = (jnp.full((TQ, 1), NEG_INF, F32), jnp.zeros((TQ, 1), F32), jnp.zeros((TQ, 2 * V_DIM), F32))
        m, l, acc = lax.fori_loop(0, i, step, init)
        vs = slice(hh * V_DIM, (hh + 1) * V_DIM)
        base = pl.multiple_of(i * TQ, TQ)
        half_rows = TQ // 2
        for half in range(2):
            rs = slice(half * half_rows, (half + 1) * half_rows)
            nk = (half + 1) * half_rows
            kb = k_ref[pl.ds(base, nk), hs]
            vb = v_ref[pl.ds(base, nk), :]
            s = lax.dot_general(q[rs, :], kb, nt, preferred_element_type=F32)
            qpos = lax.broadcasted_iota(I32, (half_rows, nk), 0) + half * half_rows
            kpos = lax.broadcasted_iota(I32, (half_rows, nk), 1)
            s = jnp.where(kpos <= qpos, s, NEG_INF)
            m_new = jnp.maximum(m[rs, :], jnp.max(s, axis=-1, keepdims=True))
            alpha = jnp.exp2(m[rs, :] - m_new)
            p = jnp.exp2(s - m_new)
            l_new = alpha * l[rs, :] + jnp.sum(p, axis=-1, keepdims=True)
            acc_new = alpha * acc[rs, :] + jnp.dot(p.astype(BF16), vb, preferred_element_type=F32)
            o_ref[rs, vs] = (acc_new / l_new)[:, vs].astype(BF16)


def _attention(q, k, v):
    nq = SEQ // TQ
    pairs = MLA_HEADS // 2
    return pl.pallas_call(
        _attn_kernel,
        grid=(BATCH, pairs, nq),
        in_specs=[pl.BlockSpec((TQ, 2 * HEAD_PAD), lambda b, p, i: (b * nq + i, p)),
                  pl.BlockSpec((SEQ, 2 * HEAD_PAD), lambda b, p, i: (b, p)),
                  pl.BlockSpec((SEQ, 2 * V_DIM), lambda b, p, i: (b, p))],
        out_specs=pl.BlockSpec((TQ, 2 * V_DIM), lambda b, p, i: (b * nq + i, p)),
        out_shape=jax.ShapeDtypeStruct((TOKENS, MLA_HEADS * V_DIM), BF16),
        compiler_params=_cparams(3),
        name="mla_attention",
    )(q, k, v)


def _route_vectors(h1, gffn, wr_hi, wr_lo, carry):
    xn = _rms(h1, gffn)
    x_hi = xn.astype(BF16)
    x_lo = (xn - x_hi.astype(F32)).astype(BF16)
    logits = (jnp.dot(x_hi, wr_hi, preferred_element_type=F32)
              + jnp.dot(x_lo, wr_hi, preferred_element_type=F32)
              + jnp.dot(x_hi, wr_lo, preferred_element_type=F32))
    lane = lax.broadcasted_iota(I32, logits.shape, 1).astype(F32)
    big = float(LANES)

    is_g = lane < N_GROUPS
    gl = jnp.where(is_g, logits, NEG_INF)
    gmax = jnp.max(gl, axis=-1, keepdims=True)
    gidx = jnp.min(jnp.where(gl == gmax, lane, big), axis=-1, keepdims=True)
    gsum = jnp.sum(jnp.where(is_g, jnp.exp(gl - gmax), 0.0), axis=-1, keepdims=True)
    g_w = 1.0 / gsum

    lo = ROUTE_LO + gidx * EXP_PER_GROUP
    el = jnp.where((lane >= lo) & (lane < lo + EXP_PER_GROUP), logits, NEG_INF)
    m1 = jnp.max(el, axis=-1, keepdims=True)
    i1 = jnp.min(jnp.where(el == m1, lane, big), axis=-1, keepdims=True)
    el2 = jnp.where(lane == i1, NEG_INF, el)
    m2 = jnp.max(el2, axis=-1, keepdims=True)
    i2 = jnp.min(jnp.where(el2 == m2, lane, big), axis=-1, keepdims=True)
    t = jnp.exp(m2 - m1)
    w1 = g_w / (1.0 + t)
    w2 = g_w * t / (1.0 + t)

    oh1 = lane == i1
    oh2 = lane == i2
    oh = jnp.where(oh1 | oh2, 1.0, 0.0)
    r = lax.broadcasted_iota(I32, (TS, TS), 0)
    cc = lax.broadcasted_iota(I32, (TS, TS), 1)
    tri = jnp.where(cc < r, 1.0, 0.0).astype(BF16)
    before = jnp.dot(tri, oh.astype(BF16), preferred_element_type=F32) + carry[...]
    rank1 = jnp.sum(jnp.where(oh1, before, 0.0), axis=-1, keepdims=True)
    rank2 = jnp.sum(jnp.where(oh2, before, 0.0), axis=-1, keepdims=True)
    carry[...] = carry[...] + jnp.sum(oh, axis=0, keepdims=True)

    hi1 = jnp.floor(rank1 * (1.0 / BM))
    hi2 = jnp.floor(rank2 * (1.0 / BM))
    vals = (i1 - ROUTE_LO, i2 - ROUTE_LO, w1, w2,
            (i1 - ROUTE_LO) * NJ + hi1, (i2 - ROUTE_LO) * NJ + hi2, rank1 - hi1 * BM, rank2 - hi2 * BM)
    route = jnp.zeros_like(logits)
    for n, val in enumerate(vals):
        route = jnp.where(lane == float(n), val, route)
    route_t = route.T
    rows = [route_t[n:n + 1, part * LANES:(part + 1) * LANES] for n in (4, 5, 6, 7) for part in range(PARTS)]
    rows.append(carry[...])
    rows.append(jnp.zeros((STG_ROWS - CNT_ROW - 1, LANES), F32))
    return xn, route, jnp.concatenate(rows, axis=0).astype(I32)


def _moe_front(mixer, gffn_ref, wrh_ref, wrl_ref, route_ref, xs_hbm, dest_out, tab_out, cnt_out, nal_out,
               carry, xns, stg_v, stg_s, tab, cntp, nal, ssem, gsem, zsem):
    i = pl.program_id(0)
    n = pl.num_programs(0)

    def stage_copy(slot):
        return pltpu.make_async_copy(stg_v, stg_s.at[slot], ssem.at[slot])

    def hand_out_chunks(slot):
        def body(e, na):
            c_new = stg_s[slot, CNT_ROW, ROUTE_LO + e]
            cb = (cntp[e] + (BM - 1)) >> BM_SHIFT
            ca = (c_new + (BM - 1)) >> BM_SHIFT
            tab[e * NJ + cb] = na
            cntp[e] = c_new
            return na + (ca - cb)

        nal[0] = lax.fori_loop(0, N_EXPERTS, body, nal[0])

    def dest_row(slot, which, part, lane):
        chunk = tab[stg_s[slot, which * PARTS + part, lane]]
        return (chunk << BM_SHIFT) + stg_s[slot, (2 + which) * PARTS + part, lane]

    def scatter_row(slot, tile, which, part, lane):
        r = part * LANES + lane
        d = dest_row(slot, which, part, lane)
        dest_out[which * DEST_HALF + (tile + SCATTER_LAG) * TS + r] = d
        _row_copy(xns[slot], r, xs_hbm, d, gsem.at[slot]).start(priority=which)

    def wait_scatter(slot):
        def body(r, c):
            _row_copy(xns[slot], 0, xs_hbm, 0, gsem.at[slot]).wait()
            return c

        lax.fori_loop(0, 2 * TS, body, 0, unroll=8)

    @pl.when(i == 0)
    def _():
        carry[...] = jnp.zeros_like(carry)
        for slot in range(X_SLOTS - SCATTER_LAG, X_SLOTS):
            xns[slot][...] = jnp.zeros((TS, D_MODEL), F32)
            for which in range(2):
                for part in range(PARTS):
                    def placeholder(l, c, slot=slot, which=which, part=part):
                        stg_s[slot, which * PARTS + part, l] = N_EXPERTS * NJ + which
                        stg_s[slot, (2 + which) * PARTS + part, l] = part * LANES + l
                        return c

                    lax.fori_loop(0, LANES, placeholder, 0)

        def clear_tab(j, c):
            tab[j] = 0
            return c

        lax.fori_loop(0, N_EXPERTS * NJ, clear_tab, 0)
        tab[N_EXPERTS * NJ] = N_CHUNKS
        tab[N_EXPERTS * NJ + 1] = N_CHUNKS + 1

        def clear_cnt(e, c):
            cntp[e] = 0
            return c

        lax.fori_loop(0, N_EXPERTS, clear_cnt, 0)
        nal[0] = 0

    def scatter_loop(slot, tile):
        for part in range(PARTS):
            def scatter_lane(lane, c, part=part):
                for which in range(2):
                    scatter_row(slot, tile, which, part, lane)
                return c

            lax.fori_loop(0, LANES, scatter_lane, 0)

    def variant(k):
        prev = (k + X_SLOTS - 1) % X_SLOTS
        lagged = (k + X_SLOTS - SCATTER_LAG) % X_SLOTS

        def run():
            @pl.when(i >= X_SLOTS)
            def _():
                wait_scatter(k)

            for part in range(PARTS):
                for lane in range(LANES):
                    for which in range(2):
                        scatter_row(lagged, i - SCATTER_LAG, which, part, lane)
            xn, route, stage = _route_vectors(mixer(), gffn_ref[...], wrh_ref[...], wrl_ref[...], carry)
            xns[k][...] = xn
            route_ref[...] = route

            @pl.when(i >= 1)
            def _():
                stage_copy(prev).wait()
                hand_out_chunks(prev)

            stg_v[...] = stage
            stage_copy(k).start()

        return run

    lax.switch(i % X_SLOTS, [variant(k) for k in range(X_SLOTS)])

    for step in range(SCATTER_LAG):
        @pl.when(i == step)
        def _():
            wait_scatter(X_SLOTS - SCATTER_LAG + step)

    @pl.when(i == n - 1)
    def _():
        for k in range(X_SLOTS):
            @pl.when(i % X_SLOTS == k)
            def _():
                scatter_loop((k + X_SLOTS - 1) % X_SLOTS, i - 1)
                stage_copy(k).wait()
                hand_out_chunks(k)
                scatter_loop(k, i)

        for k in range(X_SLOTS):
            wait_scatter(k)

        xns[0][...] = jnp.zeros((TS, D_MODEL), F32)

        def pad_expert(e, c):
            used = cntp[e] & (BM - 1)
            base = tab[e * NJ + (cntp[e] >> BM_SHIFT)] << BM_SHIFT
            stop = jnp.where(used > 0, BM, 0)
            lax.fori_loop(used, stop, lambda r, c2: (_row_copy(xns[0], 0, xs_hbm, base + r, zsem).start(), c2)[1], 0)
            lax.fori_loop(used, stop, lambda r, c2: (_row_copy(xns[0], 0, xs_hbm, base + r, zsem).wait(), c2)[1], 0)
            cnt_out[e] = cntp[e]
            return c

        lax.fori_loop(0, N_EXPERTS, pad_expert, 0)

        def spare(chunk):
            return pltpu.make_async_copy(xns[0], xs_hbm.at[_chunk_rows(chunk)], zsem)

        lax.fori_loop(nal[0], N_CHUNKS, lambda ch, c: (spare(ch).start(), c)[1], 0)
        lax.fori_loop(nal[0], N_CHUNKS, lambda ch, c: (spare(ch).wait(), c)[1], 0)

        def copy_tab(j, c):
            tab_out[j] = tab[j]
            return c

        lax.fori_loop(0, N_EXPERTS * NJ, copy_tab, 0)
        nal_out[0] = nal[0]


_FRONT_OUT_SHAPES = [jax.ShapeDtypeStruct((TOKENS, D_MODEL), F32),
                     jax.ShapeDtypeStruct((TOKENS, LANES), F32),
                     jax.ShapeDtypeStruct((XS_ROWS, D_MODEL), F32),
                     jax.ShapeDtypeStruct((2 * DEST_HALF,), I32),
                     jax.ShapeDtypeStruct((N_EXPERTS * NJ,), I32),
                     jax.ShapeDtypeStruct((N_EXPERTS,), I32),
                     jax.ShapeDtypeStruct((1,), I32)]


def _front_out_specs():
    tok = lambda w: pl.BlockSpec((TS, w), lambda i: (i, 0))
    smem = lambda: pl.BlockSpec(memory_space=pltpu.SMEM)
    return [tok(D_MODEL), tok(LANES), pl.BlockSpec(memory_space=pl.ANY), smem(), smem(), smem(), smem()]


def _front_scratch():
    return ([pltpu.VMEM((1, LANES), F32)] + [pltpu.VMEM((TS, D_MODEL), F32)] * X_SLOTS
            + [pltpu.VMEM((STG_ROWS, LANES), I32), pltpu.SMEM((X_SLOTS, STG_ROWS, LANES), I32),
               pltpu.SMEM((N_EXPERTS * NJ + 2,), I32), pltpu.SMEM((N_EXPERTS,), I32), pltpu.SMEM((1,), I32),
               pltpu.SemaphoreType.DMA((X_SLOTS,)), pltpu.SemaphoreType.DMA((X_SLOTS,)), pltpu.SemaphoreType.DMA])


def _split_front_scratch(scratch):
    carry = scratch[0]
    xns = tuple(scratch[1:1 + X_SLOTS])
    return (carry, xns) + tuple(scratch[1 + X_SLOTS:])


def _even_out_kernel(ya_ref, yb_ref, wo_ref, h_ref, gffn_ref, wrh_ref, wrl_ref,
                     h1_ref, route_ref, xs_hbm, dest_out, tab_out, cnt_out, nal_out, *scratch):
    def mixer():
        mix = (jnp.dot(ya_ref[...], wo_ref[0:CONV_CH, :], preferred_element_type=F32)
               + jnp.dot(yb_ref[...], wo_ref[CONV_CH:, :], preferred_element_type=F32))
        h1 = h_ref[...] + mix
        h1_ref[...] = h1
        return h1

    _moe_front(mixer, gffn_ref, wrh_ref, wrl_ref, route_ref, xs_hbm, dest_out, tab_out, cnt_out, nal_out,
               *_split_front_scratch(scratch))


def _even_out(ya, yb, wo, h, gffn, wrh, wrl):
    tok = lambda w: pl.BlockSpec((TS, w), lambda i: (i, 0))
    return pl.pallas_call(
        _even_out_kernel,
        grid=(N_TILES,),
        in_specs=[tok(CONV_CH), tok(MLA_HEADS * V_DIM), _full(wo.shape), tok(D_MODEL),
                  _full((1, D_MODEL)), _full(wrh.shape), _full(wrl.shape)],
        out_specs=_front_out_specs(),
        out_shape=_FRONT_OUT_SHAPES,
        scratch_shapes=_front_scratch(),
        compiler_params=_cparams(),
        name="even_out_route",
    )(ya, yb, wo, h, gffn, wrh, wrl)


def _gelu_tanh(x):
    return 0.5 * x * (1.0 + jnp.tanh(math.sqrt(2.0 / math.pi) * (x + 0.044715 * (x * x * x))))


def _odd_kernel(h_ref, g_ref, win_ref, vn_ref, ws_ref, bs_ref, wo_ref, gffn_ref, wrh_ref, wrl_ref,
                h1_ref, route_ref, xs_hbm, dest_out, tab_out, cnt_out, nal_out, mixbuf, *scratch):
    def mixer():
        h = h_ref[...]
        hn = _rms(h, g_ref[...]).astype(BF16)
        z = _gelu_tanh(jnp.dot(hn, win_ref[...], preferred_element_type=F32))
        u = z[:, :SG_WIDTH]
        v = _rms(z[:, SG_WIDTH:], vn_ref[...]).astype(BF16)
        r = lax.broadcasted_iota(I32, (SG_CHUNK, SG_CHUNK), 0)
        c = lax.broadcasted_iota(I32, (SG_CHUNK, SG_CHUNK), 1)
        n_chunks = TS // SG_CHUNK
        dg = SG_WIDTH // SG_GROUPS
        for gi in range(SG_GROUPS):
            gs = slice(gi * dg, (gi + 1) * dg)
            wtril = jnp.where(c <= r, ws_ref[gi], 0.0).astype(BF16)
            rhs = jnp.concatenate([v[ci * SG_CHUNK:(ci + 1) * SG_CHUNK, gs] for ci in range(n_chunks)], axis=1)
            sg = jnp.dot(wtril, rhs, preferred_element_type=F32)
            for ci in range(n_chunks):
                rs = slice(ci * SG_CHUNK, (ci + 1) * SG_CHUNK)
                gate = sg[:, ci * dg:(ci + 1) * dg] + bs_ref[:, gs]
                mixbuf[rs, gs] = (u[rs, gs] * gate).astype(BF16)
        h1 = h + jnp.dot(mixbuf[...], wo_ref[...], preferred_element_type=F32)
        h1_ref[...] = h1
        return h1

    _moe_front(mixer, gffn_ref, wrh_ref, wrl_ref, route_ref, xs_hbm, dest_out, tab_out, cnt_out, nal_out,
               *_split_front_scratch(scratch))


def _odd_layer(h, g, win, vn, ws, bs_full, wo, gffn, wrh, wrl):
    tok = lambda w: pl.BlockSpec((TS, w), lambda i: (i, 0))
    return pl.pallas_call(
        _odd_kernel,
        grid=(N_TILES,),
        in_specs=[tok(D_MODEL), _full((1, D_MODEL)), _full(win.shape), _full((1, SG_WIDTH)),
                  _full(ws.shape), _full(bs_full.shape), _full(wo.shape),
                  _full((1, D_MODEL)), _full(wrh.shape), _full(wrl.shape)],
        out_specs=_front_out_specs(),
        out_shape=_FRONT_OUT_SHAPES,
        scratch_shapes=[pltpu.VMEM((TS, SG_WIDTH), BF16)] + _front_scratch(),
        compiler_params=_cparams(),
        name="odd_mixer_route",
    )(h, g, win, vn, ws, bs_full, wo, gffn, wrh, wrl)


def _gmm_kernel(tab_ref, cnt_ref, nal_ref, wg_ref, wu_ref, wd_ref, xs_hbm, yb_hbm,
                wgb, wub, wdb, xbuf, ybuf, order, first, xsem, ysem, zsem):
    e = pl.program_id(0)
    nblk = nal_ref[0]

    @pl.when(e == 0)
    def _():
        def list_expert(ex, g):
            first[ex] = g
            nb = (cnt_ref[ex] + (BM - 1)) >> BM_SHIFT

            def put(j, c):
                order[g + j] = tab_ref[ex * NJ + j]
                return c

            lax.fori_loop(0, nb, put, 0)
            return g + nb

        first[N_EXPERTS] = lax.fori_loop(0, N_EXPERTS, list_expert, 0)

    def x_copy(g, slot):
        return pltpu.make_async_copy(xs_hbm.at[_chunk_rows(order[g])], xbuf.at[slot], xsem.at[slot])

    def y_copy(g, slot):
        return pltpu.make_async_copy(ybuf.at[slot], yb_hbm.at[_chunk_rows(order[g])], ysem.at[slot])

    @pl.when(e == 0)
    def _():
        for a in range(AHEAD):
            @pl.when(a < nblk)
            def _():
                x_copy(a, a).start()

    g0 = first[e]
    g1 = first[e + 1]

    @pl.when(g1 > g0)
    def _():
        wgb[...] = wg_ref[0].astype(BF16)
        wub[...] = wu_ref[0].astype(BF16)
        wdb[...] = wd_ref[0].astype(BF16)

        def block(g, c):
            slot = g % X_SLOTS
            ys = g & 1

            @pl.when(g + AHEAD < nblk)
            def _():
                x_copy(g + AHEAD, (g + AHEAD) % X_SLOTS).start()

            x_copy(g, slot).wait()

            @pl.when(g >= 2)
            def _():
                y_copy(g - 2, ys).wait()

            x = xbuf[slot].astype(BF16)
            gate = jnp.dot(x, wgb[...], preferred_element_type=F32)
            up = jnp.dot(x, wub[...], preferred_element_type=F32)
            mid = (gate * jax.nn.sigmoid(gate) * up).astype(BF16)
            ybuf[ys] = jnp.dot(mid, wdb[...], preferred_element_type=F32)
            y_copy(g, ys).start()
            return c

        lax.fori_loop(g0, g1, block, 0)

    @pl.when(e == pl.num_programs(0) - 1)
    def _():
        @pl.when(nblk >= 2)
        def _():
            y_copy(nblk - 2, nblk & 1).wait()

        y_copy(nblk - 1, (nblk - 1) & 1).wait()
        xbuf[0] = jnp.zeros((BM, D_MODEL), F32)

        def spare(chunk):
            return pltpu.make_async_copy(xbuf.at[0], yb_hbm.at[_chunk_rows(chunk)], zsem)

        lax.fori_loop(nblk, N_CHUNKS, lambda ch, c: (spare(ch).start(), c)[1], 0)
        lax.fori_loop(nblk, N_CHUNKS, lambda ch, c: (spare(ch).wait(), c)[1], 0)


def _gmm(tab, cnt, nal, xs, w_gate, w_up, w_down, layer):
    wsel = lambda e, *_: (layer * N_EXPERTS + e, 0, 0)
    hbm = pl.BlockSpec(memory_space=pl.ANY)
    return pl.pallas_call(
        _gmm_kernel,
        grid_spec=pltpu.PrefetchScalarGridSpec(
            num_scalar_prefetch=3,
            grid=(N_EXPERTS,),
            in_specs=[pl.BlockSpec((1, D_MODEL, D_EXPERT), wsel),
                      pl.BlockSpec((1, D_MODEL, D_EXPERT), wsel),
                      pl.BlockSpec((1, D_EXPERT, D_MODEL), wsel),
                      hbm],
            out_specs=hbm,
            scratch_shapes=[pltpu.VMEM((D_MODEL, D_EXPERT), BF16), pltpu.VMEM((D_MODEL, D_EXPERT), BF16),
                            pltpu.VMEM((D_EXPERT, D_MODEL), BF16),
                            pltpu.VMEM((X_SLOTS, BM, D_MODEL), F32), pltpu.VMEM((2, BM, D_MODEL), F32),
                            pltpu.SMEM((N_CHUNKS,), I32), pltpu.SMEM((N_EXPERTS + 1,), I32),
                            pltpu.SemaphoreType.DMA((X_SLOTS,)), pltpu.SemaphoreType.DMA((2,)),
                            pltpu.SemaphoreType.DMA],
        ),
        out_shape=jax.ShapeDtypeStruct((N_CHUNKS * BM, D_MODEL), F32),
        compiler_params=_cparams(),
        name="moe_grouped_matmul",
    )(tab, cnt, nal, w_gate, w_up, w_down, xs)


def _combine_ple_kernel(dest_ref, h1_ref, route_ref, p_ref, gple_ref, wgate_ref, wproj_ref, fin_ref,
                        yb_hbm, out_ref, buf0, buf1, buf2, buf3, sem, *, final):
    i = pl.program_id(0)
    n = pl.num_programs(0)
    bufs = (buf0, buf1, buf2, buf3)

    def copies(tile, r, slot):
        t = (tile + SCATTER_LAG) * TS + r
        return [_row_copy(yb_hbm, dest_ref[which * DEST_HALF + t], bufs[slot], which * TS + r, sem.at[slot])
                for which in range(2)]

    def issue_loop(tile, slot):
        def body(r, c):
            for which, cp in enumerate(copies(tile, r, slot)):
                cp.start(priority=which)
            return c

        lax.fori_loop(0, TS, body, 0)

    def issue_unrolled(tile, slot):
        for r in range(TS):
            for which, cp in enumerate(copies(tile, r, slot)):
                cp.start(priority=which)

    def wait_tile(slot):
        def body(r, c):
            _row_copy(yb_hbm, 0, bufs[slot], 0, sem.at[slot]).wait()
            return c

        lax.fori_loop(0, 2 * TS, body, 0, unroll=8)

    def compute(quarter, slot):
        rows = slice(quarter * TS, (quarter + 1) * TS)
        route = route_ref[rows, :]
        w1 = route[:, 2:3]
        w2 = route[:, 3:4]
        h2 = h1_ref[rows, :] + (w1 * bufs[slot][0:TS, :] + w2 * bufs[slot][TS:2 * TS, :])
        gate = jax.nn.sigmoid(jnp.dot(_rms(h2, gple_ref[...]).astype(BF16), wgate_ref[...],
                                      preferred_element_type=F32))
        out = h2 + gate * jnp.dot(p_ref[rows, :].astype(BF16), wproj_ref[...], preferred_element_type=F32)
        if final:
            out = _rms(out, fin_ref[...])
        out_ref[rows, :] = out

    @pl.when(i == 0)
    def _():
        for k in range(AHEAD):
            issue_loop(k, k)

    for k in range(X_SLOTS):
        tile = X_SLOTS * i + k
        ahead = jnp.minimum(tile + AHEAD, N_TILES - 1)
        wait_tile(k)
        issue_unrolled(ahead, (k + AHEAD) % X_SLOTS)
        compute(k, k)

    @pl.when(i == n - 1)
    def _():
        for k in range(AHEAD):
            wait_tile((N_TILES + k) % X_SLOTS)


def _combine_ple(dest, h1, route, p_all, layer, gple, wgate, wproj, fin, yb, final):
    nt = TOKENS // (X_SLOTS * TS)
    tok = lambda w: pl.BlockSpec((X_SLOTS * TS, w), lambda i, *_: (i, 0))
    full = lambda shape: pl.BlockSpec(shape, lambda i, *_: (0,) * len(shape))
    return pl.pallas_call(
        functools.partial(_combine_ple_kernel, final=final),
        grid_spec=pltpu.PrefetchScalarGridSpec(
            num_scalar_prefetch=1,
            grid=(nt,),
            in_specs=[tok(D_MODEL), tok(LANES),
                      pl.BlockSpec((X_SLOTS * TS, PLE_DIM), lambda i, *_: (layer * nt + i, 0)),
                      full((1, D_MODEL)), full(wgate.shape), full(wproj.shape), full((1, D_MODEL)),
                      pl.BlockSpec(memory_space=pl.ANY)],
            out_specs=tok(D_MODEL),
            scratch_shapes=[pltpu.VMEM((2 * TS, D_MODEL), F32)] * X_SLOTS + [pltpu.SemaphoreType.DMA((X_SLOTS,))],
        ),
        out_shape=jax.ShapeDtypeStruct((TOKENS, D_MODEL), F32),
        compiler_params=_cparams(),
        name="moe_combine_ple",
    )(dest, h1, route, p_all, gple, wgate, wproj, fin, yb)


def _router_weights(w_rg, w_re):
    w = jnp.zeros((D_MODEL, LANES), F32).at[:, :N_GROUPS].set(w_rg).at[:, ROUTE_LO:ROUTE_LO + N_EXPERTS].set(w_re)
    hi = w.astype(BF16)
    lo = (w - hi.astype(F32)).astype(BF16)
    return hi, lo


def _even_weights(w_in, w_q_up, w_kv_up):
    base = 3 * CONV_CH + Q_RANK + KV_RANK
    kpe_cols = jnp.zeros((D_MODEL, HEAD_PAD), F32).at[:, ROPE_LO:ROPE_LO + QK_ROPE].set(w_in[:, base:])
    win = jnp.concatenate([w_in[:, :base], kpe_cols], axis=1).astype(BF16)
    dqk = QK_NOPE + QK_ROPE
    wq = jnp.pad(w_q_up.reshape(Q_RANK, MLA_HEADS, dqk), ((0, 0), (0, 0), (0, HEAD_PAD - dqk)))
    wq = wq.reshape(Q_RANK, MLA_HEADS * HEAD_PAD).astype(BF16)
    wkv = w_kv_up.reshape(KV_RANK, MLA_HEADS, QK_NOPE + V_DIM)
    wk = jnp.pad(wkv[:, :, :QK_NOPE], ((0, 0), (0, 0), (0, HEAD_PAD - QK_NOPE))).reshape(KV_RANK, MLA_HEADS * HEAD_PAD)
    wv = wkv[:, :, QK_NOPE:].reshape(KV_RANK, MLA_HEADS * V_DIM)
    return win, wq, jnp.concatenate([wk, wv], axis=1).astype(BF16)


def kernel(x, p, positions, norm_mix, norm_ffn, w_in_e, conv_w, q_norm, w_q_up, kv_norm, w_kv_up, w_out_e,
           w_in_o, v_norm, w_s, b_s, w_out_o, w_router_group, w_router_expert, w_gate, w_up, w_down,
           norm_ple, w_ple_gate, w_ple_proj, final_norm):
    h = x.reshape(TOKENS, D_MODEL)
    p_all = p.reshape(DEPTH * TOKENS, PLE_DIM)
    wg_all = w_gate.reshape(DEPTH * N_EXPERTS, D_MODEL, D_EXPERT)
    wu_all = w_up.reshape(DEPTH * N_EXPERTS, D_MODEL, D_EXPERT)
    wd_all = w_down.reshape(DEPTH * N_EXPERTS, D_EXPERT, D_MODEL)
    ctab, stab = _rope_tables(positions)
    row = lambda a: a.reshape(1, -1)

    for i in range(DEPTH):
        j = i // 2
        wrh, wrl = _router_weights(w_router_group[i], w_router_expert[i])
        if i % 2 == 0:
            win, wq, wkv = _even_weights(w_in_e[j], w_q_up[j], w_kv_up[j])
            ya, q, k, v = _even_front(h, row(norm_mix[i]), win, conv_w[j], row(q_norm[j]), wq,
                                      row(kv_norm[j]), wkv, ctab, stab)
            yb = _attention(q, k, v)
            front = _even_out(ya, yb, w_out_e[j].astype(BF16), h, row(norm_ffn[i]), wrh, wrl)
        else:
            bs_full = jnp.repeat(b_s[j].T, SG_WIDTH // SG_GROUPS, axis=1)
            front = _odd_layer(h, row(norm_mix[i]), w_in_o[j].astype(BF16), row(v_norm[j]), w_s[j], bs_full,
                               w_out_o[j].astype(BF16), row(norm_ffn[i]), wrh, wrl)
        h1, route, xs, dest, tab, cnt, nal = front
        ys = _gmm(tab, cnt, nal, xs, wg_all, wu_all, wd_all, i)
        h = _combine_ple(dest, h1, route, p_all, i, row(norm_ple[i]), w_ple_gate[i].astype(BF16),
                         w_ple_proj[i].astype(BF16), row(final_norm), ys, final=(i == DEPTH - 1))
    return h.reshape(BATCH, SEQ, D_MODEL)
```
